```python
import math
import jax, jax.numpy as jnp
from jax import lax
import numpy as np

D_MODEL = 1024
BATCH = 8
SEQ = 4096
DEPTH = 2

CHUNK = 64
Q_BLOCK = 128
EPS = 1e-5
D_FF = ((8 * D_MODEL // 3 + 127) // 128) * 128
N_A_LAYERS = DEPTH // 2
N_B_LAYERS = DEPTH - N_A_LAYERS
SSM_EXPAND = 2
SSM_D_INNER = SSM_EXPAND * D_MODEL
SSM_HEAD_DIM = 64
SSM_HEADS = SSM_D_INNER // SSM_HEAD_DIM
SSM_GROUPS = 8
SSM_HEADS_PER_GROUP = SSM_HEADS // SSM_GROUPS
SSM_STATE = 128
SSM_CONV = 4
SSM_BLOCK = CHUNK
SSM_CONV_DIM = SSM_D_INNER + 2 * SSM_GROUPS * SSM_STATE
SSM_IN_DIM = SSM_D_INNER + SSM_CONV_DIM + SSM_HEADS
DIFF_HEAD_DIM = 64
DIFF_HEADS = D_MODEL // (2 * DIFF_HEAD_DIM)
DIFF_QK_DIM = DIFF_HEADS * 2 * DIFF_HEAD_DIM
DIFF_V_DIM = DIFF_HEADS * 2 * DIFF_HEAD_DIM
ROPE_THETA = 10000.0

kernel_name = "yoco_mamba2_diffattn_macaron"


def rms_norm(x, g):
    xf = x.astype(jnp.float32)
    y = xf * lax.rsqrt(jnp.mean(xf * xf, axis=-1, keepdims=True) + EPS)
    return (y * g.astype(jnp.float32)).astype(x.dtype)


def swiglu_ffn(h, w_in, w_out):
    gate, up = jnp.split(h @ w_in, 2, axis=-1)
    return (jax.nn.silu(gate) * up) @ w_out


def rope_tables(seq_len, dim):
    inv = ROPE_THETA ** (-jnp.arange(0, dim, 2, dtype=jnp.float32) / dim)
    ang = jnp.arange(seq_len, dtype=jnp.float32)[:, None] * inv[None, :]
    return jnp.cos(ang), jnp.sin(ang)


def apply_rope(x, cos, sin):
    xf = x.astype(jnp.float32)
    x1, x2 = jnp.split(xf, 2, axis=-1)
    c = cos[None, :, None, None, :]
    s = sin[None, :, None, None, :]
    return jnp.concatenate([x1 * c - x2 * s, x2 * c + x1 * s], axis=-1).astype(x.dtype)


def causal_depthwise_conv(x, w, b):
    y = lax.conv_general_dilated(
        x, w[:, None, :].astype(x.dtype), window_strides=(1,),
        padding=[(SSM_CONV - 1, 0)], dimension_numbers=('NWC', 'WIO', 'NWC'),
        feature_group_count=x.shape[-1])
    return y + b.astype(x.dtype)


def ssd_scan(x, dt, a, b_in, c_in):
    bsz, s = x.shape[0], x.shape[1]
    nc = s // SSM_BLOCK
    L, G, Hg, P, N = SSM_BLOCK, SSM_GROUPS, SSM_HEADS_PER_GROUP, SSM_HEAD_DIM, SSM_STATE
    x = x.reshape(bsz, nc, L, G, Hg, P)
    dt = dt.reshape(bsz, nc, L, G, Hg)
    b = b_in.reshape(bsz, nc, L, G, N)
    c = c_in.reshape(bsz, nc, L, G, N)
    a_cum = jnp.cumsum(dt * a, axis=2)
    xdt = x * dt[..., None]
    seg = a_cum[:, :, :, None] - a_cum[:, :, None, :]
    causal = jnp.tril(jnp.ones((L, L), dtype=bool))[None, None, :, :, None, None]
    decay = jnp.exp(jnp.where(causal, seg, -jnp.inf))
    cb = jnp.einsum('bclgn,bcsgn->bclsg', c, b)
    y_diag = jnp.einsum('bclsg,bclsgh,bcsghp->bclghp', cb, decay, xdt)
    decay_to_end = jnp.exp(a_cum[:, :, -1:] - a_cum)
    states = jnp.einsum('bclgn,bclgh,bclghp->bcghpn', b, decay_to_end, xdt)
    chunk_decay = jnp.exp(a_cum[:, :, -1])

    def step(h, inp):
        st, dec = inp
        return h * dec[..., None, None] + st, h

    h0 = jnp.zeros((bsz, G, Hg, P, N), jnp.float32)
    _, prev = lax.scan(step, h0, (jnp.moveaxis(states, 1, 0), jnp.moveaxis(chunk_decay, 1, 0)))
    y_off = jnp.einsum('bclgn,cbghpn,bclgh->bclghp', c, prev, jnp.exp(a_cum))
    return (y_diag + y_off).reshape(bsz, s, G, Hg, P)


def mamba2_mixer(h, w_in, conv_w, conv_b, dt_bias, a_log, d_skip, norm_g, w_out):
    bsz, s, _ = h.shape
    G, Hg, P, N = SSM_GROUPS, SSM_HEADS_PER_GROUP, SSM_HEAD_DIM, SSM_STATE
    proj = h @ w_in
    z, xbc, dt = jnp.split(proj, [SSM_D_INNER, SSM_D_INNER + SSM_CONV_DIM], axis=-1)
    xbc = jax.nn.silu(causal_depthwise_conv(xbc, conv_w, conv_b))
    xs, b_in, c_in = jnp.split(xbc, [SSM_D_INNER, SSM_D_INNER + G * N], axis=-1)
    xs = xs.astype(jnp.float32).reshape(bsz, s, G, Hg, P)
    b_in = b_in.astype(jnp.float32).reshape(bsz, s, G, N)
    c_in = c_in.astype(jnp.float32).reshape(bsz, s, G, N)
    dt = jax.nn.softplus(dt.astype(jnp.float32) + dt_bias.astype(jnp.float32)).reshape(bsz, s, G, Hg)
    a = -jnp.exp(a_log.astype(jnp.float32)).reshape(G, Hg)
    y = ssd_scan(xs, dt, a, b_in, c_in) + d_skip.astype(jnp.float32).reshape(G, Hg)[:, :, None] * xs
    yg = (y.reshape(bsz, s, SSM_D_INNER) * jax.nn.silu(z.astype(jnp.float32))).reshape(bsz, s, G, SSM_D_INNER // G)
    yg = yg * lax.rsqrt(jnp.mean(yg * yg, axis=-1, keepdims=True) + EPS)
    y = (yg.reshape(bsz, s, SSM_D_INNER) * norm_g.astype(jnp.float32)).astype(h.dtype)
    return y @ w_out


def shared_kv(x, kv_norm, w_kv, cos, sin):
    bsz, s, _ = x.shape
    kv = rms_norm(x, kv_norm) @ w_kv
    k, v = jnp.split(kv, [DIFF_QK_DIM], axis=-1)
    k = apply_rope(k.reshape(bsz, s, DIFF_HEADS, 2, DIFF_HEAD_DIM), cos, sin)
    v = v.reshape(bsz, s, DIFF_HEADS, 2 * DIFF_HEAD_DIM)
    return k, v


def diff_attention(h, k, v, w_q, lam, subln_g, w_o, lambda_init, cos, sin):
    bsz, s, _ = h.shape
    q = apply_rope((h @ w_q).reshape(bsz, s, DIFF_HEADS, 2, DIFF_HEAD_DIM), cos, sin)
    lamf = lam.astype(jnp.float32)
    lam_full = (jnp.exp(jnp.sum(lamf[0] * lamf[1])) - jnp.exp(jnp.sum(lamf[2] * lamf[3]))
                + lambda_init)
    nb = s // Q_BLOCK
    q_blocks = jnp.moveaxis(q.reshape(bsz, nb, Q_BLOCK, DIFF_HEADS, 2, DIFF_HEAD_DIM), 1, 0)
    key_chunk = jnp.arange(s) // CHUNK
    scale = DIFF_HEAD_DIM ** -0.5

    def block(args):
        qb, i = args
        q_chunk = (i * Q_BLOCK + jnp.arange(Q_BLOCK)) // CHUNK
        mask = key_chunk[None, :] <= q_chunk[:, None]
        logits = jnp.einsum('bqhmd,bkhmd->bhmqk', qb, k).astype(jnp.float32) * scale
        p = jax.nn.softmax(jnp.where(mask, logits, -jnp.inf), axis=-1)
        w = p[:, :, 0] - lam_full * p[:, :, 1]
        return jnp.einsum('bhqk,bkhe->bqhe', w.astype(v.dtype), v)

    o = lax.map(block, (q_blocks, jnp.arange(nb)))
    o = jnp.moveaxis(o, 0, 1).reshape(bsz, s, DIFF_HEADS, 2 * DIFF_HEAD_DIM)
    o = rms_norm(o, subln_g) * (1.0 - lambda_init)
    return o.reshape(bsz, s, DIFF_V_DIM) @ w_o


def setup_inputs(seed: int = 0) -> dict:
    key = jax.random.key(seed)
    ks = iter(jax.random.split(key, 32))
    f32 = jnp.float32

    def nrm(shape, fan_in):
        return jax.random.normal(next(ks), shape, f32) * (fan_in ** -0.5)

    def gain(shape):
        return 1.0 + 0.02 * jax.random.normal(next(ks), shape, f32)

    x = jax.random.normal(next(ks), (BATCH, SEQ, D_MODEL), f32)
    dt0 = jnp.exp(jax.random.uniform(next(ks), (N_A_LAYERS, SSM_HEADS), f32,
                                     math.log(1e-3), math.log(1e-1)))
    return {
        "x": x,
        "ffn1_norm": gain((DEPTH, D_MODEL)),
        "ffn1_w_in": nrm((DEPTH, D_MODEL, 2 * D_FF), D_MODEL),
        "ffn1_w_out": nrm((DEPTH, D_FF, D_MODEL), D_FF),
        "mixer_norm": gain((DEPTH, D_MODEL)),
        "ffn2_norm": gain((DEPTH, D_MODEL)),
        "ffn2_w_in": nrm((DEPTH, D_MODEL, 2 * D_FF), D_MODEL),
        "ffn2_w_out": nrm((DEPTH, D_FF, D_MODEL), D_FF),
        "ssm_w_in": nrm((N_A_LAYERS, D_MODEL, SSM_IN_DIM), D_MODEL),
        "ssm_conv_w": nrm((N_A_LAYERS, SSM_CONV, SSM_CONV_DIM), SSM_CONV),
        "ssm_conv_b": 0.02 * jax.random.normal(next(ks), (N_A_LAYERS, SSM_CONV_DIM), f32),
        "ssm_dt_bias": dt0 + jnp.log(-jnp.expm1(-dt0)),
        "ssm_a_log": jnp.log(jax.random.uniform(next(ks), (N_A_LAYERS, SSM_HEADS), f32, 1.0, 16.0)),
        "ssm_d": 1.0 + 0.1 * jax.random.normal(next(ks), (N_A_LAYERS, SSM_HEADS), f32),
        "ssm_norm": gain((N_A_LAYERS, SSM_D_INNER)),
        "ssm_w_out": nrm((N_A_LAYERS, SSM_D_INNER, D_MODEL), SSM_D_INNER),
        "kv_norm": gain((D_MODEL,)),
        "w_kv": nrm((D_MODEL, DIFF_QK_DIM + DIFF_V_DIM), D_MODEL),
        "attn_w_q": nrm((N_B_LAYERS, D_MODEL, DIFF_QK_DIM), D_MODEL),
        "attn_lambda": 0.1 * jax.random.normal(next(ks), (N_B_LAYERS, 4, DIFF_HEAD_DIM), f32),
        "attn_subln": gain((N_B_LAYERS, 2 * DIFF_HEAD_DIM)),
        "attn_w_o": nrm((N_B_LAYERS, DIFF_V_DIM, D_MODEL), DIFF_V_DIM),
        "final_norm": gain((D_MODEL,)),
    }


def reference(x, ffn1_norm, ffn1_w_in, ffn1_w_out, mixer_norm, ffn2_norm, ffn2_w_in,
              ffn2_w_out, ssm_w_in, ssm_conv_w, ssm_conv_b, ssm_dt_bias, ssm_a_log, ssm_d,
              ssm_norm, ssm_w_out, kv_norm, w_kv, attn_w_q, attn_lambda, attn_subln,
              attn_w_o, final_norm):
    s = x.shape[1]
    cos, sin = rope_tables(s, DIFF_HEAD_DIM)
    shared_k, shared_v = None, None
    for layer in range(DEPTH):
        x = x + 0.5 * swiglu_ffn(rms_norm(x, ffn1_norm[layer]), ffn1_w_in[layer], ffn1_w_out[layer])
        hm = rms_norm(x, mixer_norm[layer])
        if layer < N_A_LAYERS:
            i = layer
            x = x + mamba2_mixer(hm, ssm_w_in[i], ssm_conv_w[i], ssm_conv_b[i], ssm_dt_bias[i],
                                 ssm_a_log[i], ssm_d[i], ssm_norm[i], ssm_w_out[i])
        else:
            j = layer - N_A_LAYERS
            lambda_init = 0.8 - 0.6 * math.exp(-0.3 * layer)
            x = x + diff_attention(hm, shared_k, shared_v, attn_w_q[j], attn_lambda[j],
                                   attn_subln[j], attn_w_o[j], lambda_init, cos, sin)
        x = x + 0.5 * swiglu_ffn(rms_norm(x, ffn2_norm[layer]), ffn2_w_in[layer], ffn2_w_out[layer])
        if layer == N_A_LAYERS - 1:
            shared_k, shared_v = shared_kv(x, kv_norm, w_kv, cos, sin)
    return rms_norm(x, final_norm)
```

```python
import functools
import math

import jax
import jax.numpy as jnp
from jax import lax
from jax.experimental import pallas as pl
from jax.experimental.pallas import tpu as pltpu

F32 = jnp.float32
BF16 = jnp.bfloat16

EPS = 1e-5
CHUNK = 64
SSM_HEAD_DIM = 64
SSM_GROUPS = 8
SSM_STATE = 128
SSM_CONV = 4
DIFF_HEAD_DIM = 64
ROPE_THETA = 10000.0
LANES = 128
SUBLANES = 8
VMEM_LIMIT_BYTES = 58 * 1024 * 1024


def _resident(shape):
    zeros = (0,) * len(shape)
    return pl.BlockSpec(shape, lambda *_: zeros, pipeline_mode=pl.Buffered(1))


def _rms(x, g):
    ms = jnp.mean(x * x, axis=-1, keepdims=True)
    return x * lax.rsqrt(ms + EPS) * g


def _silu(x):
    return x * jax.nn.sigmoid(x)


def _dot(a, b):
    return jnp.dot(a, b, preferred_element_type=F32)


def _dot_nt(a, b):
    return lax.dot_general(a, b, (((1,), (1,)), ((), ())), preferred_element_type=F32)


def _rope(xb, cos, sin_signed):
    lane = lax.broadcasted_iota(jnp.int32, xb.shape, 1)
    first_half = (lane & (DIFF_HEAD_DIM - 1)) < (DIFF_HEAD_DIM // 2)
    rot = jnp.where(first_half,
                    pltpu.roll(xb, LANES - DIFF_HEAD_DIM // 2, 1),
                    pltpu.roll(xb, DIFF_HEAD_DIM // 2, 1))
    return xb * cos + rot * sin_signed


def _ffn_body(*refs, n_chunks, final_norm):
    if final_norm:
        x_ref, g_ref, win_ref, wout_ref, fg_ref, o_ref, xn_ref = refs
    else:
        x_ref, g_ref, win_ref, wout_ref, o_ref, xn_ref = refs
    x = x_ref[...]
    xn_ref[...] = _rms(x, g_ref[...]).astype(BF16)
    o_ref[...] = x

    def chunk(j, carry):
        xn = xn_ref[...]
        gate = _dot(xn, win_ref[j])
        up = _dot(xn, win_ref[n_chunks + j])
        h = (_silu(gate) * up * 0.5).astype(BF16)
        o_ref[...] += _dot(h, wout_ref[j])
        return carry

    lax.fori_loop(0, n_chunks, chunk, 0)
    if final_norm:
        o_ref[...] = _rms(o_ref[...], fg_ref[...])


def _ffn(x, norm_g, w_in, w_out, final_g=None, *, tm=512, tf=256):
    t, d = x.shape
    f = w_out.shape[0]
    tm = min(tm, t)
    n_chunks = f // tf
    assert f % tf == 0 and t % tm == 0
    win = w_in.astype(BF16).reshape(d, 2 * n_chunks, tf).transpose(1, 0, 2)
    wout = w_out.astype(BF16).reshape(n_chunks, tf, d)
    final_norm = final_g is not None
    operands = [x, norm_g.reshape(1, d), win, wout]
    in_specs = [pl.BlockSpec((tm, d), lambda i: (i, 0)),
                _resident((1, d)),
                _resident((2 * n_chunks, d, tf)),
                _resident((n_chunks, tf, d))]
    if final_norm:
        operands.append(final_g.reshape(1, d))
        in_specs.append(_resident((1, d)))
    return pl.pallas_call(
        functools.partial(_ffn_body, n_chunks=n_chunks, final_norm=final_norm),
        grid=(t // tm,),
        in_specs=in_specs,
        out_specs=pl.BlockSpec((tm, d), lambda i: (i, 0)),
        out_shape=jax.ShapeDtypeStruct((t, d), F32),
        scratch_shapes=[pltpu.VMEM((tm, d), BF16)],
        compiler_params=pltpu.CompilerParams(
            dimension_semantics=("arbitrary",), vmem_limit_bytes=VMEM_LIMIT_BYTES),
        name="ffn_final" if final_norm else "ffn",
    )(*operands)


def _ssm_in_body(x_ref, g_ref, w_ref, z_ref, xbc_ref, dt_ref, *, d_inner, conv_dim):
    xn = _rms(x_ref[...], g_ref[...]).astype(BF16)
    z_ref[...] = _dot(xn, w_ref[:, 0:d_inner])
    xbc_ref[...] = _dot(xn, w_ref[:, d_inner:d_inner + conv_dim])
    dt_ref[...] = _dot(xn, w_ref[:, d_inner + conv_dim:d_inner + conv_dim + LANES])


def _ssm_in_proj(x, norm_g, w_in, *, d_inner, conv_dim, heads, tm=256):
    t, d = x.shape
    tm = min(tm, t)
    n_pad = d_inner + conv_dim + LANES
    w = jnp.pad(w_in.astype(BF16), ((0, 0), (0, LANES - heads)))
    return pl.pallas_call(
        functools.partial(_ssm_in_body, d_inner=d_inner, conv_dim=conv_dim),
        grid=(t // tm,),
        in_specs=[pl.BlockSpec((tm, d), lambda i: (i, 0)),
                  _resident((1, d)),
                  _resident((d, n_pad))],
        out_specs=[pl.BlockSpec((tm, d_inner), lambda i: (i, 0)),
                   pl.BlockSpec((tm, conv_dim), lambda i: (i, 0)),
                   pl.BlockSpec((tm, LANES), lambda i: (i, 0))],
        out_shape=[jax.ShapeDtypeStruct((t, d_inner), F32),
                   jax.ShapeDtypeStruct((t, conv_dim), F32),
                   jax.ShapeDtypeStruct((t, LANES), F32)],
        compiler_params=pltpu.CompilerParams(
            dimension_semantics=("arbitrary",), vmem_limit_bytes=VMEM_LIMIT_BYTES),
        name="ssm_in_proj",
    )(x, norm_g.reshape(1, d), w)


def _expand_heads(v, e_ref):
    hi = v.astype(BF16)
    lo = (v - hi.astype(F32)).astype(BF16)
    e = e_ref[...]
    return _dot(hi, e) + _dot(lo, e)


def _ssd_body(x_ref, z_ref, xbc_ref, dtp_ref, cw_ref, cb_ref, dtb_ref, alog_ref, dsk_ref,
              ng_ref, e_ref, wout_ref, o_ref,
              ext_ref, act_ref, y_ref, xdt_ref, xdtd_ref, expa_ref, acum_ref, state_ref,
              *, ts, d_inner, n_groups, n_state):
    s_idx = pl.program_id(1)
    gw = d_inner // n_groups
    conv_dim = d_inner + 2 * n_groups * n_state
    b_off = d_inner
    c_off = d_inner + n_groups * n_state
    n_chunks = ts // CHUNK

    @pl.when(s_idx == 0)
    def _():
        ext_ref[0:SUBLANES, :] = jnp.zeros((SUBLANES, conv_dim), F32)
        state_ref[...] = jnp.zeros(state_ref.shape, F32)

    @pl.when(s_idx > 0)
    def _():
        ext_ref[0:SUBLANES, :] = ext_ref[ts:ts + SUBLANES, :]

    ext_ref[SUBLANES:SUBLANES + ts, :] = xbc_ref[...]
    cblk = 512
    for j in range(conv_dim // cblk):
        cols = slice(j * cblk, (j + 1) * cblk)
        acc = cb_ref[:, cols] + cw_ref[SSM_CONV - 1:SSM_CONV, cols] * ext_ref[SUBLANES:SUBLANES + ts, cols]
        for k in range(SSM_CONV - 1):
            start = SUBLANES - (SSM_CONV - 1) + k
            acc = acc + cw_ref[k:k + 1, cols] * ext_ref[start:start + ts, cols]
        act_ref[:, cols] = _silu(acc)

    dtv = dtp_ref[...] + dtb_ref[...]
    dtv = jnp.maximum(dtv, 0.0) + jnp.log1p(jnp.exp(-jnp.abs(dtv)))
    a_neg = -jnp.exp(alog_ref[...])
    dta = dtv * a_neg
    row = lax.broadcasted_iota(jnp.int32, (ts, ts), 0)
    col = lax.broadcasted_iota(jnp.int32, (ts, ts), 1)
    same_chunk = (row >> 6) == (col >> 6)
    tri = jnp.where(same_chunk & (col <= row), 1.0, 0.0).astype(F32)
    a_cum = jnp.dot(tri, dta, precision=lax.Precision.HIGHEST, preferred_element_type=F32)
    acum_ref[...] = a_cum
    a_last = jnp.concatenate(
        [jnp.broadcast_to(a_cum[c * CHUNK + CHUNK - 1:c * CHUNK + CHUNK, :], (CHUNK, LANES))
         for c in range(n_chunks)], axis=0)
    exp_a = jnp.exp(a_cum)
    dte = jnp.exp(a_last - a_cum)

    expa_ref[...] = _expand_heads(exp_a, e_ref)
    xdt = act_ref[:, 0:d_inner] * _expand_heads(dtv, e_ref)
    xdt_ref[...] = xdt.astype(BF16)
    xdtd_ref[...] = (xdt * _expand_heads(dte, e_ref)).astype(BF16)

    lane = lax.broadcasted_iota(jnp.int32, (CHUNK, LANES), 1)
    left = lane < SSM_HEAD_DIM
    li = lax.broadcasted_iota(jnp.int32, (CHUNK, CHUNK), 0)
    si = lax.broadcasted_iota(jnp.int32, (CHUNK, CHUNK), 1)
    causal = si <= li
    zpad = jnp.zeros((LANES - CHUNK, LANES), F32)

    def chunk_body(c, carry):
        r0 = pl.multiple_of(c * CHUNK, CHUNK)
        rows = pl.ds(r0, CHUNK)
        a_c = acum_ref[rows, :]
        a_t = jnp.concatenate([a_c, zpad], axis=0).T
        for g in range(n_groups):
            gcols = slice(g * gw, (g + 1) * gw)
            b_g = act_ref[rows, b_off + g * n_state:b_off + (g + 1) * n_state]
            c_g = act_ref[rows, c_off + g * n_state:c_off + (g + 1) * n_state]
            c_b = c_g.astype(BF16)
            cb = _dot_nt(c_b, b_g.astype(BF16))
            halves = []
            for pair in range(2):
                xp = xdt_ref[rows, g * gw + pair * LANES:g * gw + (pair + 1) * LANES]
                yp = None
                for sub in range(2):
                    hh = g * 4 + pair * 2 + sub
                    seg = a_c[:, hh:hh + 1] - a_t[hh:hh + 1, 0:CHUNK]
                    m_h = (cb * jnp.exp(jnp.where(causal, seg, -jnp.inf))).astype(BF16)
                    keep = left if sub == 0 else jnp.logical_not(left)
                    contrib = _dot(m_h, jnp.where(keep, xp, jnp.zeros_like(xp)))
                    yp = contrib if yp is None else yp + contrib
                halves.append(yp)
            y_diag = jnp.concatenate(halves, axis=1)
            prev = state_ref[g]
            y_off = _dot(c_b, prev.astype(BF16)) * expa_ref[rows, gcols]
            y_ref[rows, gcols] = y_diag + y_off
            b_t = jnp.concatenate([b_g, zpad], axis=0).T[:, 0:CHUNK].astype(BF16)
            upd = _dot(b_t, xdtd_ref[rows, gcols])
            cdec = expa_ref[pl.ds(r0 + CHUNK - 1, 1), gcols]
            state_ref[g] = prev * cdec + upd
        return carry

    lax.fori_loop(0, n_chunks, chunk_body, 0)

    for g in range(n_groups):
        gcols = slice(g * gw, (g + 1) * gw)
        y = y_ref[:, gcols] + dsk_ref[:, gcols] * act_ref[:, gcols]
        yg = y * _silu(z_ref[:, gcols])
        ms = jnp.mean(yg * yg, axis=-1, keepdims=True)
        xdt_ref[:, gcols] = (yg * lax.rsqrt(ms + EPS) * ng_ref[:, gcols]).astype(BF16)
    o_ref[...] = x_ref[...] + _dot(xdt_ref[...], wout_ref[...])


def _ssd_mixer(x, z, xbc, dtp, conv_w, conv_b, dt_bias, a_log, d_skip, norm_g, w_out, *, batch, ts=256):
    t, d = x.shape
    seq = t // batch
    ts = min(ts, seq)
    assert seq % ts == 0 and ts % CHUNK == 0
    d_inner = z.shape[1]
    conv_dim = xbc.shape[1]
    heads = d_inner // SSM_HEAD_DIM
    n_groups, n_state = SSM_GROUPS, SSM_STATE
    ns = seq // ts
    pad_h = LANES - heads
    e_mat = (jnp.arange(LANES)[:, None] == (jnp.arange(d_inner)[None, :] // SSM_HEAD_DIM)).astype(BF16)
    operands = [
        x, z, xbc, dtp,
        conv_w, conv_b.reshape(1, conv_dim),
        jnp.pad(dt_bias, (0, pad_h)).reshape(1, LANES),
        jnp.pad(a_log, (0, pad_h)).reshape(1, LANES),
        jnp.repeat(d_skip, SSM_HEAD_DIM).reshape(1, d_inner),
        norm_g.reshape(1, d_inner),
        e_mat,
        w_out.astype(BF16),
    ]
    row_map = lambda b, s: (b * ns + s, 0)
    in_specs = [pl.BlockSpec((ts, d), row_map),
                pl.BlockSpec((ts, d_inner), row_map),
                pl.BlockSpec((ts, conv_dim), row_map),
                pl.BlockSpec((ts, LANES), row_map),
                _resident((SSM_CONV, conv_dim)),
                _resident((1, conv_dim)),
                _resident((1, LANES)),
                _resident((1, LANES)),
                _resident((1, d_inner)),
                _resident((1, d_inner)),
                _resident((LANES, d_inner)),
                _resident((d_inner, d))]
    return pl.pallas_call(
        functools.partial(_ssd_body, ts=ts, d_inner=d_inner, n_groups=n_groups, n_state=n_state),
        grid=(batch, ns),
        in_specs=in_specs,
        out_specs=pl.BlockSpec((ts, d), row_map),
        out_shape=jax.ShapeDtypeStruct((t, d), F32),
        scratch_shapes=[pltpu.VMEM((ts + SUBLANES, conv_dim), F32),
                        pltpu.VMEM((ts, conv_dim), F32),
                        pltpu.VMEM((ts, d_inner), F32),
                        pltpu.VMEM((ts, d_inner), BF16),
                        pltpu.VMEM((ts, d_inner), BF16),
                        pltpu.VMEM((ts, d_inner), F32),
                        pltpu.VMEM((ts, LANES), F32),
                        pltpu.VMEM((n_groups, n_state, d_inner // n_groups), F32)],
        compiler_params=pltpu.CompilerParams(
            dimension_semantics=("arbitrary", "arbitrary"), vmem_limit_bytes=VMEM_LIMIT_BYTES),
        name="ssd_mixer",
    )(*operands)


def _kv_body(x_ref, g_ref, w_ref, cos_ref, sin_ref, k_ref, v_ref, *, qk_dim):
    xn = _rms(x_ref[...], g_ref[...]).astype(BF16)
    cos = cos_ref[...]
    sin = sin_ref[...]
    for j in range(qk_dim // LANES):
        kb = _dot(xn, w_ref[:, j * LANES:(j + 1) * LANES])
        k_ref[:, j * LANES:(j + 1) * LANES] = _rope(kb, cos, sin).astype(BF16)
    v_ref[...] = _dot(xn, w_ref[:, qk_dim:]).astype(BF16)


def _shared_kv(x, norm_g, w_kv, cos_t, sin_t, *, qk_dim, seq, tm=512):
    t, d = x.shape
    tm = min(tm, seq)
    v_dim = w_kv.shape[1] - qk_dim
    n_seq_tiles = seq // tm
    return pl.pallas_call(
        functools.partial(_kv_body, qk_dim=qk_dim),
        grid=(t // tm,),
        in_specs=[pl.BlockSpec((tm, d), lambda i: (i, 0)),
                  _resident((1, d)),
                  _resident((d, qk_dim + v_dim)),
                  pl.BlockSpec((tm, LANES), lambda i: (i % n_seq_tiles, 0)),
                  pl.BlockSpec((tm, LANES), lambda i: (i % n_seq_tiles, 0))],
        out_specs=[pl.BlockSpec((tm, qk_dim), lambda i: (i, 0)),
                   pl.BlockSpec((tm, v_dim), lambda i: (i, 0))],
        out_shape=[jax.ShapeDtypeStruct((t, qk_dim), BF16),
                   jax.ShapeDtypeStruct((t, v_dim), BF16)],
        compiler_params=pltpu.CompilerParams(
            dimension_semantics=("arbitrary",), vmem_limit_bytes=VMEM_LIMIT_BYTES),
        name="shared_kv",
    )(x, norm_g.reshape(1, d), w_kv.astype(BF16), cos_t, sin_t)


def _attn_body(x_ref, g_ref, wq_ref, cos_ref, sin_ref, k_ref, v_ref, lam_ref, sg_ref, wo_ref, o_ref,
               qs_ref, m_ref, l_ref, acc_ref, os_ref, *, tq, n_heads, lambda_init):
    qi = pl.program_id(1)
    hd = 2 * DIFF_HEAD_DIM
    scale = DIFF_HEAD_DIM ** -0.5

    xn = _rms(x_ref[...], g_ref[...]).astype(BF16)
    cos = cos_ref[...]
    sin = sin_ref[...]
    lane = lax.broadcasted_iota(jnp.int32, (tq, hd), 1)
    map0 = lane < DIFF_HEAD_DIM
    for h in range(n_heads):
        qb = _dot(xn, wq_ref[:, h * hd:(h + 1) * hd])
        qr = _rope(qb, cos, sin) * scale
        qs_ref[h, 0:tq, :] = jnp.where(map0, qr, 0.0).astype(BF16)
        qs_ref[h, tq:2 * tq, :] = jnp.where(map0, 0.0, qr).astype(BF16)

    lam = lam_ref[...]
    t1 = jnp.sum(lam[0:1] * lam[1:2], axis=1, keepdims=True)
    t2 = jnp.sum(lam[2:3] * lam[3:4], axis=1, keepdims=True)
    lam_full = jnp.exp(t1) - jnp.exp(t2) + lambda_init

    r_chunk = (lax.broadcasted_iota(jnp.int32, (2 * tq, tq), 0) & (tq - 1)) >> 6
    c_chunk = lax.broadcasted_iota(jnp.int32, (2 * tq, tq), 1) >> 6
    diag_mask = c_chunk <= r_chunk

    for h in range(n_heads):
        hcols = slice(h * hd, (h + 1) * hd)
        m_ref[...] = jnp.full(m_ref.shape, -jnp.inf, F32)
        l_ref[...] = jnp.zeros(l_ref.shape, F32)
        acc_ref[...] = jnp.zeros(acc_ref.shape, F32)

        def kv_step(j, masked, hcols=hcols, h=h):
            rows = pl.ds(pl.multiple_of(j * tq, tq), tq)
            kb = k_ref[rows, hcols]
            vb = v_ref[rows, hcols]
            s = _dot_nt(qs_ref[h], kb)
            if masked:
                s = jnp.where(diag_mask, s, -jnp.inf)
            m_prev = m_ref[...]
            m_next = jnp.maximum(m_prev, jnp.max(s, axis=1, keepdims=True))
            p = jnp.exp(s - pltpu.repeat(m_next, tq // LANES, 1))
            alpha = jnp.exp(m_prev - m_next)
            l_ref[...] = alpha * l_ref[...] + jnp.sum(p, axis=1, keepdims=True)
            m_ref[...] = m_next
            acc_ref[...] = alpha * acc_ref[...] + _dot(p.astype(BF16), vb)

        def loop_body(j, carry):
            kv_step(j, False)
            return carry

        lax.fori_loop(0, qi, loop_body, 0)
        kv_step(qi, True)

        inv_l = 1.0 / l_ref[...]
        o_all = acc_ref[...] * inv_l
        o = o_all[0:tq] - lam_full * o_all[tq:2 * tq]
        ms = jnp.mean(o * o, axis=-1, keepdims=True)
        o = o * lax.rsqrt(ms + EPS) * sg_ref[...] * (1.0 - lambda_init)
        os_ref[:, hcols] = o.astype(BF16)

    o_ref[...] = x_ref[...] + _dot(os_ref[...], wo_ref[...])


def _diff_attn(x, norm_g, w_q, cos_t, sin_t, k, v, lam, subln_g, w_o, *, batch, lambda_init, tq=256):
    t, d = x.shape
    seq = t // batch
    tq = min(tq, seq)
    assert seq % tq == 0 and tq % LANES == 0 and (tq & (tq - 1)) == 0
    qk_dim = w_q.shape[1]
    hd = 2 * DIFF_HEAD_DIM
    n_heads = qk_dim // hd
    nq = seq // tq
    row_map = lambda b, i: (b * nq + i, 0)
    return pl.pallas_call(
        functools.partial(_attn_body, tq=tq, n_heads=n_heads, lambda_init=lambda_init),
        grid=(batch, nq),
        in_specs=[pl.BlockSpec((tq, d), row_map),
                  _resident((1, d)),
                  _resident((d, qk_dim)),
                  pl.BlockSpec((tq, LANES), lambda b, i: (i, 0)),
                  pl.BlockSpec((tq, LANES), lambda b, i: (i, 0)),
                  pl.BlockSpec((seq, qk_dim), lambda b, i: (b, 0)),
                  pl.BlockSpec((seq, v.shape[1]), lambda b, i: (b, 0)),
                  _resident(lam.shape),
                  _resident((1, hd)),
                  _resident((v.shape[1], d))],
        out_specs=pl.BlockSpec((tq, d), row_map),
        out_shape=jax.ShapeDtypeStruct((t, d), F32),
        scratch_shapes=[pltpu.VMEM((n_heads, 2 * tq, hd), BF16),
                        pltpu.VMEM((2 * tq, LANES), F32),
                        pltpu.VMEM((2 * tq, LANES), F32),
                        pltpu.VMEM((2 * tq, hd), F32),
                        pltpu.VMEM((tq, n_heads * hd), BF16)],
        compiler_params=pltpu.CompilerParams(
            dimension_semantics=("arbitrary", "arbitrary"), vmem_limit_bytes=VMEM_LIMIT_BYTES),
        name="diff_attn",
    )(x, norm_g.reshape(1, d), w_q.astype(BF16), cos_t, sin_t, k, v, lam,
      subln_g.reshape(1, hd), w_o.astype(BF16))


def _rope_tables(seq):
    dim = DIFF_HEAD_DIM
    inv = ROPE_THETA ** (-jnp.arange(0, dim, 2, dtype=F32) / dim)
    ang = jnp.arange(seq, dtype=F32)[:, None] * inv[None, :]
    cos, sin = jnp.cos(ang), jnp.sin(ang)
    cos_t = jnp.tile(cos, (1, 2 * LANES // dim))
    sin_t = jnp.tile(jnp.concatenate([-sin, sin], axis=1), (1, LANES // dim))
    return cos_t, sin_t


def kernel(x, ffn1_norm, ffn1_w_in, ffn1_w_out, mixer_norm, ffn2_norm, ffn2_w_in, ffn2_w_out, ssm_w_in, ssm_conv_w, ssm_conv_b, ssm_dt_bias, ssm_a_log, ssm_d, ssm_norm, ssm_w_out, kv_norm, w_kv, attn_w_q, attn_lambda, attn_subln, attn_w_o, final_norm):
    batch, seq, d = x.shape
    depth = ffn1_norm.shape[0]
    n_a = ssm_w_in.shape[0]
    d_inner = ssm_w_out.shape[1]
    heads = ssm_dt_bias.shape[1]
    conv_dim = ssm_conv_w.shape[2]
    qk_dim = attn_w_q.shape[2]
    cos_t, sin_t = _rope_tables(seq)
    h = x.reshape(batch * seq, d)
    k = v = None
    for layer in range(depth):
        h = _ffn(h, ffn1_norm[layer], ffn1_w_in[layer], ffn1_w_out[layer])
        if layer < n_a:
            i = layer
            z, xbc, dtp = _ssm_in_proj(h, mixer_norm[layer], ssm_w_in[i],
                                       d_inner=d_inner, conv_dim=conv_dim, heads=heads)
            h = _ssd_mixer(h, z, xbc, dtp, ssm_conv_w[i], ssm_conv_b[i], ssm_dt_bias[i], ssm_a_log[i],
                           ssm_d[i], ssm_norm[i], ssm_w_out[i], batch=batch)
        else:
            j = layer - n_a
            lambda_init = 0.8 - 0.6 * math.exp(-0.3 * layer)
            h = _diff_attn(h, mixer_norm[layer], attn_w_q[j], cos_t, sin_t, k, v, attn_lambda[j],
                           attn_subln[j], attn_w_o[j], batch=batch, lambda_init=lambda_init)
        last = layer == depth - 1
        h = _ffn(h, ffn2_norm[layer], ffn2_w_in[layer], ffn2_w_out[layer],
                 final_norm if last else None)
        if layer == n_a - 1:
            k, v = _shared_kv(h, kv_norm, w_kv, cos_t, sin_t, qk_dim=qk_dim, seq=seq)
    return h.reshape(batch, seq, d)
```

```python
import functools
import math

import jax
import jax.numpy as jnp
from jax import lax
from jax.experimental import pallas as pl
from jax.experimental.pallas import tpu as pltpu

F32 = jnp.float32
BF16 = jnp.bfloat16

EPS = 1e-5
CHUNK = 64
SSM_HEAD_DIM = 64
SSM_GROUPS = 8
SSM_STATE = 128
SSM_CONV = 4
DIFF_HEAD_DIM = 64
ROPE_THETA = 10000.0
LANES = 128
SUBLANES = 8
VMEM_LIMIT_BYTES = 58 * 1024 * 1024


def _resident(shape):
    zeros = (0,) * len(shape)
    return pl.BlockSpec(shape, lambda *_: zeros, pipeline_mode=pl.Buffered(1))


def _rms(x, g):
    ms = jnp.mean(x * x, axis=-1, keepdims=True)
    return x * lax.rsqrt(ms + EPS) * g


def _silu(x):
    return x * jax.nn.sigmoid(x)


def _dot(a, b):
    return jnp.dot(a, b, preferred_element_type=F32)


def _dot_nt(a, b):
    return lax.dot_general(a, b, (((1,), (1,)), ((), ())), preferred_element_type=F32)


def _rope(xb, cos, sin_signed):
    lane = lax.broadcasted_iota(jnp.int32, xb.shape, 1)
    first_half = (lane & (DIFF_HEAD_DIM - 1)) < (DIFF_HEAD_DIM // 2)
    rot = jnp.where(first_half,
                    pltpu.roll(xb, LANES - DIFF_HEAD_DIM // 2, 1),
                    pltpu.roll(xb, DIFF_HEAD_DIM // 2, 1))
    return xb * cos + rot * sin_signed


def _ffn_body(*refs, n_chunks, final_norm):
    if final_norm:
        x_ref, g_ref, win_ref, wout_ref, fg_ref, o_ref, xn_ref = refs
    else:
        x_ref, g_ref, win_ref, wout_ref, o_ref, xn_ref = refs
    x = x_ref[...]
    xn_ref[...] = _rms(x, g_ref[...]).astype(BF16)
    o_ref[...] = x

    def chunk(j, carry):
        xn = xn_ref[...]
        gate = _dot(xn, win_ref[j])
        up = _dot(xn, win_ref[n_chunks + j])
        h = (_silu(gate) * up * 0.5).astype(BF16)
        o_ref[...] += _dot(h, wout_ref[j])
        return carry

    lax.fori_loop(0, n_chunks, chunk, 0)
    if final_norm:
        o_ref[...] = _rms(o_ref[...], fg_ref[...])


def _ffn(x, norm_g, w_in, w_out, final_g=None, *, tm=512, tf=256):
    t, d = x.shape
    f = w_out.shape[0]
    tm = min(tm, t)
    n_chunks = f // tf
    assert f % tf == 0 and t % tm == 0
    win = w_in.astype(BF16).reshape(d, 2 * n_chunks, tf).transpose(1, 0, 2)
    wout = w_out.astype(BF16).reshape(n_chunks, tf, d)
    final_norm = final_g is not None
    operands = [x, norm_g.reshape(1, d), win, wout]
    in_specs = [pl.BlockSpec((tm, d), lambda i: (i, 0)),
                _resident((1, d)),
                _resident((2 * n_chunks, d, tf)),
                _resident((n_chunks, tf, d))]
    if final_norm:
        operands.append(final_g.reshape(1, d))
        in_specs.append(_resident((1, d)))
    return pl.pallas_call(
        functools.partial(_ffn_body, n_chunks=n_chunks, final_norm=final_norm),
        grid=(t // tm,),
        in_specs=in_specs,
        out_specs=pl.BlockSpec((tm, d), lambda i: (i, 0)),
        out_shape=jax.ShapeDtypeStruct((t, d), F32),
        scratch_shapes=[pltpu.VMEM((tm, d), BF16)],
        compiler_params=pltpu.CompilerParams(
            dimension_semantics=("arbitrary",), vmem_limit_bytes=VMEM_LIMIT_BYTES),
        name="ffn_final" if final_norm else "ffn",
    )(*operands)


def _ssm_in_body(x_ref, g_ref, w_ref, z_ref, xbc_ref, dt_ref, *, d_inner, conv_dim):
    xn = _rms(x_ref[...], g_ref[...]).astype(BF16)
    z_ref[...] = _dot(xn, w_ref[:, 0:d_inner])
    xbc_ref[...] = _dot(xn, w_ref[:, d_inner:d_inner + conv_dim])
    dt_ref[...] = _dot(xn, w_ref[:, d_inner + conv_dim:d_inner + conv_dim + LANES])


def _ssm_in_proj(x, norm_g, w_in, *, d_inner, conv_dim, heads, tm=256):
    t, d = x.shape
    tm = min(tm, t)
    n_pad = d_inner + conv_dim + LANES
    w = jnp.pad(w_in.astype(BF16), ((0, 0), (0, LANES - heads)))
    return pl.pallas_call(
        functools.partial(_ssm_in_body, d_inner=d_inner, conv_dim=conv_dim),
        grid=(t // tm,),
        in_specs=[pl.BlockSpec((tm, d), lambda i: (i, 0)),
                  _resident((1, d)),
                  _resident((d, n_pad))],
        out_specs=[pl.BlockSpec((tm, d_inner), lambda i: (i, 0)),
                   pl.BlockSpec((tm, conv_dim), lambda i: (i, 0)),
                   pl.BlockSpec((tm, LANES), lambda i: (i, 0))],
        out_shape=[jax.ShapeDtypeStruct((t, d_inner), F32),
                   jax.ShapeDtypeStruct((t, conv_dim), F32),
                   jax.ShapeDtypeStruct((t, LANES), F32)],
        compiler_params=pltpu.CompilerParams(
            dimension_semantics=("arbitrary",), vmem_limit_bytes=VMEM_LIMIT_BYTES),
        name="ssm_in_proj",
    )(x, norm_g.reshape(1, d), w)


def _expand_heads(v, e_ref):
    hi = v.astype(BF16)
    lo = (v - hi.astype(F32)).astype(BF16)
    e = e_ref[...]
    return _dot(hi, e) + _dot(lo, e)


def _ssd_body(x_ref, z_ref, xbc_ref, dtp_ref, cw_ref, cb_ref, dtb_ref, alog_ref, dsk_ref,
              ng_ref, e_ref, wout_ref, o_ref,
              ext_ref, act_ref, y_ref, xdt_ref, xdtd_ref, expa_ref, acum_ref, state_ref,
              *, ts, d_inner, n_groups, n_state):
    s_idx = pl.program_id(1)
    gw = d_inner // n_groups
    conv_dim = d_inner + 2 * n_groups * n_state
    b_off = d_inner
    c_off = d_inner + n_groups * n_state
    n_chunks = ts // CHUNK

    @pl.when(s_idx == 0)
    def _():
        ext_ref[0:SUBLANES, :] = jnp.zeros((SUBLANES, conv_dim), F32)
        state_ref[...] = jnp.zeros(state_ref.shape, F32)

    @pl.when(s_idx > 0)
    def _():
        ext_ref[0:SUBLANES, :] = ext_ref[ts:ts + SUBLANES, :]

    ext_ref[SUBLANES:SUBLANES + ts, :] = xbc_ref[...]
    cblk = 512
    for j in range(conv_dim // cblk):
        cols = slice(j * cblk, (j + 1) * cblk)
        acc = cb_ref[:, cols] + cw_ref[SSM_CONV - 1:SSM_CONV, cols] * ext_ref[SUBLANES:SUBLANES + ts, cols]
        for k in range(SSM_CONV - 1):
            start = SUBLANES - (SSM_CONV - 1) + k
            acc = acc + cw_ref[k:k + 1, cols] * ext_ref[start:start + ts, cols]
        act_ref[:, cols] = _silu(acc)

    dtv = dtp_ref[...] + dtb_ref[...]
    dtv = jnp.maximum(dtv, 0.0) + jnp.log1p(jnp.exp(-jnp.abs(dtv)))
    a_neg = -jnp.exp(alog_ref[...])
    dta = dtv * a_neg
    row = lax.broadcasted_iota(jnp.int32, (ts, ts), 0)
    col = lax.broadcasted_iota(jnp.int32, (ts, ts), 1)
    same_chunk = (row >> 6) == (col >> 6)
    tri = jnp.where(same_chunk & (col <= row), 1.0, 0.0).astype(F32)
    a_cum = jnp.dot(tri, dta, precision=lax.Precision.HIGHEST, preferred_element_type=F32)
    acum_ref[...] = a_cum
    a_last = jnp.concatenate(
        [jnp.broadcast_to(a_cum[c * CHUNK + CHUNK - 1:c * CHUNK + CHUNK, :], (CHUNK, LANES))
         for c in range(n_chunks)], axis=0)
    exp_a = jnp.exp(a_cum)
    dte = jnp.exp(a_last - a_cum)

    expa_ref[...] = _expand_heads(exp_a, e_ref)
    xdt = act_ref[:, 0:d_inner] * _expand_heads(dtv, e_ref)
    xdt_ref[...] = xdt.astype(BF16)
    xdtd_ref[...] = (xdt * _expand_heads(dte, e_ref)).astype(BF16)

    lane = lax.broadcasted_iota(jnp.int32, (CHUNK, LANES), 1)
    left = lane < SSM_HEAD_DIM
    li = lax.broadcasted_iota(jnp.int32, (CHUNK, CHUNK), 0)
    si = lax.broadcasted_iota(jnp.int32, (CHUNK, CHUNK), 1)
    causal = si <= li
    zpad = jnp.zeros((LANES - CHUNK, LANES), F32)

    def chunk_body(c, carry):
        r0 = pl.multiple_of(c * CHUNK, CHUNK)
        rows = pl.ds(r0, CHUNK)
        a_c = acum_ref[rows, :]
        a_t = jnp.concatenate([a_c, zpad], axis=0).T
        for g in range(n_groups):
            gcols = slice(g * gw, (g + 1) * gw)
            b_g = act_ref[rows, b_off + g * n_state:b_off + (g + 1) * n_state]
            c_g = act_ref[rows, c_off + g * n_state:c_off + (g + 1) * n_state]
            c_b = c_g.astype(BF16)
            cb = _dot_nt(c_b, b_g.astype(BF16))
            halves = []
            for pair in range(2):
                xp = xdt_ref[rows, g * gw + pair * LANES:g * gw + (pair + 1) * LANES]
                yp = None
                for sub in range(2):
                    hh = g * 4 + pair * 2 + sub
                    seg = a_c[:, hh:hh + 1] - a_t[hh:hh + 1, 0:CHUNK]
                    m_h = (cb * jnp.exp(jnp.where(causal, seg, -jnp.inf))).astype(BF16)
                    keep = left if sub == 0 else jnp.logical_not(left)
                    contrib = _dot(m_h, jnp.where(keep, xp, jnp.zeros_like(xp)))
                    yp = contrib if yp is None else yp + contrib
                halves.append(yp)
            y_diag = jnp.concatenate(halves, axis=1)
            prev = state_ref[g]
            y_off = _dot(c_b, prev.astype(BF16)) * expa_ref[rows, gcols]
            y_ref[rows, gcols] = y_diag + y_off
            b_t = jnp.concatenate([b_g, zpad], axis=0).T[:, 0:CHUNK].astype(BF16)
            upd = _dot(b_t, xdtd_ref[rows, gcols])
            cdec = expa_ref[pl.ds(r0 + CHUNK - 1, 1), gcols]
            state_ref[g] = prev * cdec + upd
        return carry

    lax.fori_loop(0, n_chunks, chunk_body, 0)

    for g in range(n_groups):
        gcols = slice(g * gw, (g + 1) * gw)
        y = y_ref[:, gcols] + dsk_ref[:, gcols] * act_ref[:, gcols]
        yg = y * _silu(z_ref[:, gcols])
        ms = jnp.mean(yg * yg, axis=-1, keepdims=True)
        xdt_ref[:, gcols] = (yg * lax.rsqrt(ms + EPS) * ng_ref[:, gcols]).astype(BF16)
    o_ref[...] = x_ref[...] + _dot(xdt_ref[...], wout_ref[...])


def _ssd_mixer(x, z, xbc, dtp, conv_w, conv_b, dt_bias, a_log, d_skip, norm_g, w_out, *, batch, ts=256):
    t, d = x.shape
    seq = t // batch
    ts = min(ts, seq)
    assert seq % ts == 0 and ts % CHUNK == 0
    d_inner = z.shape[1]
    conv_dim = xbc.shape[1]
    heads = d_inner // SSM_HEAD_DIM
    n_groups, n_state = SSM_GROUPS, SSM_STATE
    ns = seq // ts
    pad_h = LANES - heads
    e_mat = (jnp.arange(LANES)[:, None] == (jnp.arange(d_inner)[None, :] // SSM_HEAD_DIM)).astype(BF16)
    operands = [
        x, z, xbc, dtp,
        conv_w, conv_b.reshape(1, conv_dim),
        jnp.pad(dt_bias, (0, pad_h)).reshape(1, LANES),
        jnp.pad(a_log, (0, pad_h)).reshape(1, LANES),
        jnp.repeat(d_skip, SSM_HEAD_DIM).reshape(1, d_inner),
        norm_g.reshape(1, d_inner),
        e_mat,
        w_out.astype(BF16),
    ]
    row_map = lambda b, s: (b * ns + s, 0)
    in_specs = [pl.BlockSpec((ts, d), row_map),
                pl.BlockSpec((ts, d_inner), row_map),
                pl.BlockSpec((ts, conv_dim), row_map),
                pl.BlockSpec((ts, LANES), row_map),
                _resident((SSM_CONV, conv_dim)),
                _resident((1, conv_dim)),
                _resident((1, LANES)),
                _resident((1, LANES)),
                _resident((1, d_inner)),
                _resident((1, d_inner)),
                _resident((LANES, d_inner)),
                _resident((d_inner, d))]
    return pl.pallas_call(
        functools.partial(_ssd_body, ts=ts, d_inner=d_inner, n_groups=n_groups, n_state=n_state),
        grid=(batch, ns),
        in_specs=in_specs,
        out_specs=pl.BlockSpec((ts, d), row_map),
        out_shape=jax.ShapeDtypeStruct((t, d), F32),
        scratch_shapes=[pltpu.VMEM((ts + SUBLANES, conv_dim), F32),
                        pltpu.VMEM((ts, conv_dim), F32),
                        pltpu.VMEM((ts, d_inner), F32),
                        pltpu.VMEM((ts, d_inner), BF16),
                        pltpu.VMEM((ts, d_inner), BF16),
                        pltpu.VMEM((ts, d_inner), F32),
                        pltpu.VMEM((ts, LANES), F32),
                        pltpu.VMEM((n_groups, n_state, d_inner // n_groups), F32)],
        compiler_params=pltpu.CompilerParams(
            dimension_semantics=("arbitrary", "arbitrary"), vmem_limit_bytes=VMEM_LIMIT_BYTES),
        name="ssd_mixer",
    )(*operands)


def _kv_body(x_ref, g_ref, w_ref, cos_ref, sin_ref, k_ref, v_ref, *, qk_dim):
    xn = _rms(x_ref[...], g_ref[...]).astype(BF16)
    cos = cos_ref[...]
    sin = sin_ref[...]
    for j in range(qk_dim // LANES):
        kb = _dot(xn, w_ref[:, j * LANES:(j + 1) * LANES])
        k_ref[:, j * LANES:(j + 1) * LANES] = _rope(kb, cos, sin).astype(BF16)
    v_ref[...] = _dot(xn, w_ref[:, qk_dim:]).astype(BF16)


def _shared_kv(x, norm_g, w_kv, cos_t, sin_t, *, qk_dim, seq, tm=512):
    t, d = x.shape
    tm = min(tm, seq)
    v_dim = w_kv.shape[1] - qk_dim
    n_seq_tiles = seq // tm
    return pl.pallas_call(
        functools.partial(_kv_body, qk_dim=qk_dim),
        grid=(t // tm,),
        in_specs=[pl.BlockSpec((tm, d), lambda i: (i, 0)),
                  _resident((1, d)),
                  _resident((d, qk_dim + v_dim)),
                  pl.BlockSpec((tm, LANES), lambda i: (i % n_seq_tiles, 0)),
                  pl.BlockSpec((tm, LANES), lambda i: (i % n_seq_tiles, 0))],
        out_specs=[pl.BlockSpec((tm, qk_dim), lambda i: (i, 0)),
                   pl.BlockSpec((tm, v_dim), lambda i: (i, 0))],
        out_shape=[jax.ShapeDtypeStruct((t, qk_dim), BF16),
                   jax.ShapeDtypeStruct((t, v_dim), BF16)],
        compiler_params=pltpu.CompilerParams(
            dimension_semantics=("arbitrary",), vmem_limit_bytes=VMEM_LIMIT_BYTES),
        name="shared_kv",
    )(x, norm_g.reshape(1, d), w_kv.astype(BF16), cos_t, sin_t)


def _attn_body(x_ref, g_ref, wq_ref, cos_ref, sin_ref, k_ref, v_ref, lam_ref, sg_ref, wo_ref, o_ref,
               qs_ref, m_ref, l_ref, acc_ref, os_ref, *, tq, n_heads, lambda_init):
    qi = pl.program_id(1)
    hd = 2 * DIFF_HEAD_DIM
    scale = DIFF_HEAD_DIM ** -0.5 * math.log2(math.e)

    xn = _rms(x_ref[...], g_ref[...]).astype(BF16)
    cos = cos_ref[...]
    sin = sin_ref[...]
    lane = lax.broadcasted_iota(jnp.int32, (tq, hd), 1)
    map0 = lane < DIFF_HEAD_DIM
    for h in range(n_heads):
        qb = _dot(xn, wq_ref[:, h * hd:(h + 1) * hd])
        qr = _rope(qb, cos, sin) * scale
        qs_ref[h, 0:tq, :] = jnp.where(map0, qr, 0.0).astype(BF16)
        qs_ref[h, tq:2 * tq, :] = jnp.where(map0, 0.0, qr).astype(BF16)

    lam = lam_ref[...]
    t1 = jnp.sum(lam[0:1] * lam[1:2], axis=1, keepdims=True)
    t2 = jnp.sum(lam[2:3] * lam[3:4], axis=1, keepdims=True)
    lam_full = jnp.exp(t1) - jnp.exp(t2) + lambda_init

    r_chunk = (lax.broadcasted_iota(jnp.int32, (2 * tq, tq), 0) & (tq - 1)) >> 6
    c_chunk = lax.broadcasted_iota(jnp.int32, (2 * tq, tq), 1) >> 6
    diag_mask = c_chunk <= r_chunk

    m_ref[...] = jnp.full(m_ref.shape, -jnp.inf, F32)
    l_ref[...] = jnp.zeros(l_ref.shape, F32)
    acc_ref[...] = jnp.zeros(acc_ref.shape, F32)

    def kv_block(j, masked):
        rows = pl.ds(pl.multiple_of(j * tq, tq), tq)
        for h in range(n_heads):
            hcols = slice(h * hd, (h + 1) * hd)
            kb = k_ref[rows, hcols]
            vb = v_ref[rows, hcols]
            s = _dot_nt(qs_ref[h], kb)
            if masked:
                s = jnp.where(diag_mask, s, -jnp.inf)
            m_prev = m_ref[h]
            m_next = jnp.maximum(m_prev, jnp.max(s, axis=1, keepdims=True))
            p = jnp.exp2(s - pltpu.repeat(m_next, tq // LANES, 1))
            alpha = jnp.exp2(m_prev - m_next)
            l_ref[h] = alpha * l_ref[h] + jnp.sum(p, axis=1, keepdims=True)
            m_ref[h] = m_next
            acc_ref[h] = alpha * acc_ref[h] + _dot(p.astype(BF16), vb)

    def loop_body(j, carry):
        kv_block(j, False)
        return carry

    lax.fori_loop(0, qi, loop_body, 0)
    kv_block(qi, True)

    for h in range(n_heads):
        hcols = slice(h * hd, (h + 1) * hd)
        inv_l = 1.0 / l_ref[h]
        o_all = acc_ref[h] * inv_l
        o = o_all[0:tq] - lam_full * o_all[tq:2 * tq]
        ms = jnp.mean(o * o, axis=-1, keepdims=True)
        o = o * lax.rsqrt(ms + EPS) * sg_ref[...] * (1.0 - lambda_init)
        os_ref[:, hcols] = o.astype(BF16)

    o_ref[...] = x_ref[...] + _dot(os_ref[...], wo_ref[...])


def _diff_attn(x, norm_g, w_q, cos_t, sin_t, k, v, lam, subln_g, w_o, *, batch, lambda_init, tq=256):
    t, d = x.shape
    seq = t // batch
    tq = min(tq, seq)
    assert seq % tq == 0 and tq % LANES == 0 and (tq & (tq - 1)) == 0
    qk_dim = w_q.shape[1]
    hd = 2 * DIFF_HEAD_DIM
    n_heads = qk_dim // hd
    nq = seq // tq
    row_map = lambda b, i: (b * nq + i, 0)
    return pl.pallas_call(
        functools.partial(_attn_body, tq=tq, n_heads=n_heads, lambda_init=lambda_init),
        grid=(batch, nq),
        in_specs=[pl.BlockSpec((tq, d), row_map),
                  _resident((1, d)),
                  _resident((d, qk_dim)),
                  pl.BlockSpec((tq, LANES), lambda b, i: (i, 0)),
                  pl.BlockSpec((tq, LANES), lambda b, i: (i, 0)),
                  pl.BlockSpec((seq, qk_dim), lambda b, i: (b, 0)),
                  pl.BlockSpec((seq, v.shape[1]), lambda b, i: (b, 0)),
                  _resident(lam.shape),
                  _resident((1, hd)),
                  _resident((v.shape[1], d))],
        out_specs=pl.BlockSpec((tq, d), row_map),
        out_shape=jax.ShapeDtypeStruct((t, d), F32),
        scratch_shapes=[pltpu.VMEM((n_heads, 2 * tq, hd), BF16),
                        pltpu.VMEM((n_heads, 2 * tq, LANES), F32),
                        pltpu.VMEM((n_heads, 2 * tq, LANES), F32),
                        pltpu.VMEM((n_heads, 2 * tq, hd), F32),
                        pltpu.VMEM((tq, n_heads * hd), BF16)],
        compiler_params=pltpu.CompilerParams(
            dimension_semantics=("arbitrary", "arbitrary"), vmem_limit_bytes=VMEM_LIMIT_BYTES),
        name="diff_attn",
    )(x, norm_g.reshape(1, d), w_q.astype(BF16), cos_t, sin_t, k, v, lam,
      subln_g.reshape(1, hd), w_o.astype(BF16))


def _rope_tables(seq):
    dim = DIFF_HEAD_DIM
    inv = ROPE_THETA ** (-jnp.arange(0, dim, 2, dtype=F32) / dim)
    ang = jnp.arange(seq, dtype=F32)[:, None] * inv[None, :]
    cos, sin = jnp.cos(ang), jnp.sin(ang)
    cos_t = jnp.tile(cos, (1, 2 * LANES // dim))
    sin_t = jnp.tile(jnp.concatenate([-sin, sin], axis=1), (1, LANES // dim))
    return cos_t, sin_t


def kernel(x, ffn1_norm, ffn1_w_in, ffn1_w_out, mixer_norm, ffn2_norm, ffn2_w_in, ffn2_w_out, ssm_w_in, ssm_conv_w, ssm_conv_b, ssm_dt_bias, ssm_a_log, ssm_d, ssm_norm, ssm_w_out, kv_norm, w_kv, attn_w_q, attn_lambda, attn_subln, attn_w_o, final_norm):
    batch, seq, d = x.shape
    depth = ffn1_norm.shape[0]
    n_a = ssm_w_in.shape[0]
    d_inner = ssm_w_out.shape[1]
    heads = ssm_dt_bias.shape[1]
    conv_dim = ssm_conv_w.shape[2]
    qk_dim = attn_w_q.shape[2]
    cos_t, sin_t = _rope_tables(seq)
    h = x.reshape(batch * seq, d)
    k = v = None
    for layer in range(depth):
        h = _ffn(h, ffn1_norm[layer], ffn1_w_in[layer], ffn1_w_out[layer])
        if layer < n_a:
            i = layer
            z, xbc, dtp = _ssm_in_proj(h, mixer_norm[layer], ssm_w_in[i],
                                       d_inner=d_inner, conv_dim=conv_dim, heads=heads)
            h = _ssd_mixer(h, z, xbc, dtp, ssm_conv_w[i], ssm_conv_b[i], ssm_dt_bias[i], ssm_a_log[i],
                           ssm_d[i], ssm_norm[i], ssm_w_out[i], batch=batch)
        else:
            j = layer - n_a
            lambda_init = 0.8 - 0.6 * math.exp(-0.3 * layer)
            h = _diff_attn(h, mixer_norm[layer], attn_w_q[j], cos_t, sin_t, k, v, attn_lambda[j],
                           attn_subln[j], attn_w_o[j], batch=batch, lambda_init=lambda_init)
        last = layer == depth - 1
        h = _ffn(h, ffn2_norm[layer], ffn2_w_in[layer], ffn2_w_out[layer],
                 final_norm if last else None)
        if layer == n_a - 1:
            k, v = _shared_kv(h, kv_norm, w_kv, cos_t, sin_t, qk_dim=qk_dim, seq=seq)
    return h.reshape(batch, seq, d)
```

```python
import functools
import math

import jax
import jax.numpy as jnp
from jax import lax
from jax.experimental import pallas as pl
from jax.experimental.pallas import tpu as pltpu

F32 = jnp.float32
BF16 = jnp.bfloat16

EPS = 1e-5
CHUNK = 64
CHUNK_SHIFT = CHUNK.bit_length() - 1
SSM_HEAD_DIM = 64
SSM_GROUPS = 8
SSM_STATE = 128
SSM_CONV = 4
SSD_BLOCK = 128
DIFF_HEAD_DIM = 64
V_ONES_ROWS = 16
ROPE_THETA = 10000.0
LANES = 128
SUBLANES = 8
VMEM_LIMIT_BYTES = 58 * 1024 * 1024


def _resident(shape):
    zeros = (0,) * len(shape)
    return pl.BlockSpec(shape, lambda *_: zeros, pipeline_mode=pl.Buffered(1))


def _rms(x, g):
    ms = jnp.mean(x * x, axis=-1, keepdims=True)
    return x * lax.rsqrt(ms + EPS) * g


def _silu(x):
    return x * jax.nn.sigmoid(x)


def _dot(a, b):
    return jnp.dot(a, b, preferred_element_type=F32)


def _dot_nt(a, b):
    return lax.dot_general(a, b, (((1,), (1,)), ((), ())), preferred_element_type=F32)


def _dot_tn(a, b):
    return lax.dot_general(a, b, (((0,), (0,)), ((), ())), preferred_element_type=F32)


def _ffn_body(*refs, d_ff, tf, final_norm):
    if final_norm:
        x_ref, g_ref, win_ref, wout_ref, fg_ref, o_ref, xn_ref = refs
    else:
        x_ref, g_ref, win_ref, wout_ref, o_ref, xn_ref = refs
    x = x_ref[...]
    xn_ref[...] = _rms(x, g_ref[...]).astype(BF16)
    o_ref[...] = x
    for j in range(d_ff // tf):
        xn = xn_ref[...]
        gate = _dot(xn, win_ref[:, j * tf:(j + 1) * tf])
        up = _dot(xn, win_ref[:, d_ff + j * tf:d_ff + (j + 1) * tf])
        h = (_silu(gate) * up * 0.5).astype(BF16)
        o_ref[...] += _dot(h, wout_ref[j * tf:(j + 1) * tf, :])
    if final_norm:
        o_ref[...] = _rms(o_ref[...], fg_ref[...])


def _ffn(x, norm_g, w_in, w_out, final_g=None, *, tm=512, tf=256):
    t, d = x.shape
    d_ff = w_out.shape[0]
    tm = min(tm, t)
    assert d_ff % tf == 0 and t % tm == 0
    final_norm = final_g is not None
    operands = [x, norm_g.reshape(1, d), w_in.astype(BF16), w_out.astype(BF16)]
    in_specs = [pl.BlockSpec((tm, d), lambda i: (i, 0)),
                _resident((1, d)),
                _resident((d, 2 * d_ff)),
                _resident((d_ff, d))]
    if final_norm:
        operands.append(final_g.reshape(1, d))
        in_specs.append(_resident((1, d)))
    return pl.pallas_call(
        functools.partial(_ffn_body, d_ff=d_ff, tf=tf, final_norm=final_norm),
        grid=(t // tm,),
        in_specs=in_specs,
        out_specs=pl.BlockSpec((tm, d), lambda i: (i, 0)),
        out_shape=jax.ShapeDtypeStruct((t, d), F32),
        scratch_shapes=[pltpu.VMEM((tm, d), BF16)],
        compiler_params=pltpu.CompilerParams(
            dimension_semantics=("arbitrary",), vmem_limit_bytes=VMEM_LIMIT_BYTES),
        name="ffn_final" if final_norm else "ffn",
    )(*operands)


def _expand_heads(v, e2_ref):
    hi = v.astype(BF16)
    lo = (v - hi.astype(F32)).astype(BF16)
    return _dot(jnp.concatenate([hi, lo], axis=1), e2_ref[...])


def _mamba_body(x_ref, g_ref, win_ref, cw_ref, cb_ref, dtb_ref, alog_ref, dsk_ref, ng_ref, e2_ref,
                wout_ref, o_ref,
                xn_ref, ext_ref, xs_ref, xsb_ref, bc_ref, y_ref, xdtd_ref, expa_ref, acum_ref, pt_ref,
                state_ref, *, ts, d_inner, n_groups, n_state, n_heads):
    s_idx = pl.program_id(1)
    gw = d_inner // n_groups
    hpg = gw // SSM_HEAD_DIM
    bc_dim = n_groups * n_state
    conv_dim = d_inner + 2 * bc_dim
    blk = SSD_BLOCK
    n_blocks = ts // blk

    xn_ref[...] = _rms(x_ref[...], g_ref[...]).astype(BF16)

    @pl.when(s_idx == 0)
    def _():
        ext_ref[0:SUBLANES, :] = jnp.zeros((SUBLANES, conv_dim), F32)
        state_ref[...] = jnp.zeros(state_ref.shape, F32)

    @pl.when(s_idx > 0)
    def _():
        ext_ref[0:SUBLANES, :] = ext_ref[ts:ts + SUBLANES, :]

    cblk = 512
    for j in range(conv_dim // cblk):
        cols = slice(j * cblk, (j + 1) * cblk)
        ext_ref[SUBLANES:SUBLANES + ts, cols] = _dot(xn_ref[...], win_ref[:, d_inner + j * cblk:d_inner + (j + 1) * cblk])

    for j in range(conv_dim // cblk):
        cols = slice(j * cblk, (j + 1) * cblk)
        acc = cb_ref[:, cols] + cw_ref[SSM_CONV - 1:SSM_CONV, cols] * ext_ref[SUBLANES:SUBLANES + ts, cols]
        for k in range(SSM_CONV - 1):
            start = SUBLANES - (SSM_CONV - 1) + k
            acc = acc + cw_ref[k:k + 1, cols] * ext_ref[start:start + ts, cols]
        a = _silu(acc)
        if j * cblk < d_inner:
            xs_ref[:, cols] = a
            xsb_ref[:, cols] = a.astype(BF16)
        else:
            bc_ref[:, j * cblk - d_inner:(j + 1) * cblk - d_inner] = a.astype(BF16)

    dtv = _dot(xn_ref[...], win_ref[:, d_inner + conv_dim:d_inner + conv_dim + LANES]) + dtb_ref[...]
    dtv = jnp.maximum(dtv, 0.0) + jnp.log1p(jnp.exp(-jnp.abs(dtv)))
    dta = dtv * (-jnp.exp(alog_ref[...]))
    row = lax.broadcasted_iota(jnp.int32, (ts, ts), 0)
    col = lax.broadcasted_iota(jnp.int32, (ts, ts), 1)
    shift = blk.bit_length() - 1
    tri = jnp.where(((row >> shift) == (col >> shift)) & (col <= row), 1.0, 0.0).astype(F32)
    a_cum = jnp.dot(tri, dta, precision=lax.Precision.HIGHEST, preferred_element_type=F32)
    acum_ref[...] = a_cum
    a_last = jnp.concatenate(
        [jnp.broadcast_to(a_cum[c * blk + blk - 1:c * blk + blk, :], (blk, LANES)) for c in range(n_blocks)],
        axis=0)
    expa_ref[...] = _expand_heads(jnp.exp(a_cum), e2_ref)
    xdtd_ref[...] = (xs_ref[...] * _expand_heads(dtv * jnp.exp(a_last - a_cum), e2_ref)).astype(BF16)
    lane_h = lax.broadcasted_iota(jnp.int32, (ts, LANES), 1)
    pt_ref[...] = jnp.where(lane_h < n_heads, a_cum, pltpu.roll(dtv, n_heads, 1)).T

    li = lax.broadcasted_iota(jnp.int32, (blk, blk), 0)
    si = lax.broadcasted_iota(jnp.int32, (blk, blk), 1)
    causal = si <= li
    head_of_lane = lax.broadcasted_iota(jnp.int32, (blk, gw), 1) >> (SSM_HEAD_DIM.bit_length() - 1)

    for c in range(n_blocks):
        rows = slice(c * blk, (c + 1) * blk)
        for g in range(n_groups):
            gcols = slice(g * gw, (g + 1) * gw)
            b_g = bc_ref[rows, g * n_state:(g + 1) * n_state]
            c_g = bc_ref[rows, bc_dim + g * n_state:bc_dim + (g + 1) * n_state]
            cb = _dot_nt(c_g, b_g)
            xg = xsb_ref[rows, gcols]
            m_parts, x_parts = [], []
            for h4 in range(hpg):
                hh = g * hpg + h4
                seg = acum_ref[rows, hh:hh + 1] - pt_ref[hh:hh + 1, rows]
                decay = jnp.exp(jnp.where(causal, seg, -jnp.inf))
                m_parts.append((cb * pt_ref[n_heads + hh:n_heads + hh + 1, rows] * decay).astype(BF16))
                x_parts.append(jnp.where(head_of_lane == h4, xg, jnp.zeros_like(xg)))
            y_diag = _dot(jnp.concatenate(m_parts, axis=1), jnp.concatenate(x_parts, axis=0))
            prev = state_ref[g]
            y_off = _dot(c_g, prev.astype(BF16)) * expa_ref[rows, gcols]
            y_ref[rows, gcols] = y_diag + y_off
            upd = _dot_tn(b_g, xdtd_ref[rows, gcols])
            state_ref[g] = prev * expa_ref[(c + 1) * blk - 1:(c + 1) * blk, gcols] + upd

    for g in range(n_groups):
        gcols = slice(g * gw, (g + 1) * gw)
        z = _dot(xn_ref[...], win_ref[:, gcols])
        y = y_ref[:, gcols] + dsk_ref[:, gcols] * xs_ref[:, gcols]
        yg = y * _silu(z)
        ms = jnp.mean(yg * yg, axis=-1, keepdims=True)
        xsb_ref[:, gcols] = (yg * lax.rsqrt(ms + EPS) * ng_ref[:, gcols]).astype(BF16)
    o_ref[...] = x_ref[...] + _dot(xsb_ref[...], wout_ref[...])


def _mamba_mixer(x, norm_g, w_in, conv_w, conv_b, dt_bias, a_log, d_skip, ssm_norm_g, w_out, *, batch, ts=256):
    t, d = x.shape
    seq = t // batch
    ts = min(ts, seq)
    d_inner = w_out.shape[0]
    conv_dim = conv_w.shape[1]
    heads = dt_bias.shape[0]
    n_groups, n_state = SSM_GROUPS, SSM_STATE
    assert seq % ts == 0 and ts % SSD_BLOCK == 0 and 2 * heads <= LANES
    assert conv_dim == d_inner + 2 * n_groups * n_state and heads * SSM_HEAD_DIM == d_inner
    ns = seq // ts
    pad_h = LANES - heads
    n_pad = d_inner + conv_dim + LANES
    e_mat = (jnp.arange(LANES)[:, None] == (jnp.arange(d_inner)[None, :] // SSM_HEAD_DIM)).astype(BF16)
    operands = [
        x, norm_g.reshape(1, d),
        jnp.pad(w_in.astype(BF16), ((0, 0), (0, pad_h))),
        conv_w, conv_b.reshape(1, conv_dim),
        jnp.pad(dt_bias, (0, pad_h)).reshape(1, LANES),
        jnp.pad(a_log, (0, pad_h)).reshape(1, LANES),
        jnp.repeat(d_skip, SSM_HEAD_DIM).reshape(1, d_inner),
        ssm_norm_g.reshape(1, d_inner),
        jnp.concatenate([e_mat, e_mat], axis=0),
        w_out.astype(BF16),
    ]
    row_map = lambda b, s: (b * ns + s, 0)
    in_specs = [pl.BlockSpec((ts, d), row_map),
                _resident((1, d)),
                _resident((d, n_pad)),
                _resident((SSM_CONV, conv_dim)),
                _resident((1, conv_dim)),
                _resident((1, LANES)),
                _resident((1, LANES)),
                _resident((1, d_inner)),
                _resident((1, d_inner)),
                _resident((2 * LANES, d_inner)),
                _resident((d_inner, d))]
    return pl.pallas_call(
        functools.partial(_mamba_body, ts=ts, d_inner=d_inner, n_groups=n_groups, n_state=n_state,
                          n_heads=heads),
        grid=(batch, ns),
        in_specs=in_specs,
        out_specs=pl.BlockSpec((ts, d), row_map),
        out_shape=jax.ShapeDtypeStruct((t, d), F32),
        scratch_shapes=[pltpu.VMEM((ts, d), BF16),
                        pltpu.VMEM((ts + SUBLANES, conv_dim), F32),
                        pltpu.VMEM((ts, d_inner), F32),
                        pltpu.VMEM((ts, d_inner), BF16),
                        pltpu.VMEM((ts, 2 * n_groups * n_state), BF16),
                        pltpu.VMEM((ts, d_inner), F32),
                        pltpu.VMEM((ts, d_inner), BF16),
                        pltpu.VMEM((ts, d_inner), F32),
                        pltpu.VMEM((ts, LANES), F32),
                        pltpu.VMEM((LANES, ts), F32),
                        pltpu.VMEM((n_groups, n_state, d_inner // n_groups), F32)],
        compiler_params=pltpu.CompilerParams(
            dimension_semantics=("arbitrary", "arbitrary"), vmem_limit_bytes=VMEM_LIMIT_BYTES),
        name="mamba_mixer",
    )(*operands)


def _rope_rows(xb, cos, sin_signed):
    lane = lax.broadcasted_iota(jnp.int32, xb.shape, 1)
    first_half = (lane & (DIFF_HEAD_DIM - 1)) < (DIFF_HEAD_DIM // 2)
    rot = jnp.where(first_half,
                    pltpu.roll(xb, LANES - DIFF_HEAD_DIM // 2, 1),
                    pltpu.roll(xb, DIFF_HEAD_DIM // 2, 1))
    return xb * cos + rot * sin_signed


def _rope_cols(xt, cos_t, sin_signed_t):
    half = DIFF_HEAD_DIM // 2
    parts = []
    for u in range(xt.shape[0] // DIFF_HEAD_DIM):
        base = u * DIFF_HEAD_DIM
        parts += [xt[base + half:base + 2 * half], xt[base:base + half]]
    return xt * cos_t + jnp.concatenate(parts, axis=0) * sin_signed_t


def _kv_body(x_ref, g_ref, wk_ref, wvt_ref, cos_ref, sin_ref, k_ref, vt_ref, *, qk_dim):
    xn = _rms(x_ref[...], g_ref[...]).astype(BF16)
    cos = cos_ref[...]
    sin = sin_ref[...]
    for j in range(qk_dim // LANES):
        kb = _dot(xn, wk_ref[:, j * LANES:(j + 1) * LANES])
        k_ref[:, j * LANES:(j + 1) * LANES] = _rope_rows(kb, cos, sin).astype(BF16)
    hd = 2 * DIFF_HEAD_DIM
    ones = jnp.ones((V_ONES_ROWS, xn.shape[0]), BF16)
    for h in range(wvt_ref.shape[0] // hd):
        base = h * (hd + V_ONES_ROWS)
        vt_ref[0, base:base + hd, :] = _dot_nt(wvt_ref[h * hd:(h + 1) * hd, :], xn).astype(BF16)
        vt_ref[0, base + hd:base + hd + V_ONES_ROWS, :] = ones


def _shared_kv(x, norm_g, w_kv, cos_t, sin_t, *, qk_dim, seq, tk):
    t, d = x.shape
    v_dim = w_kv.shape[1] - qk_dim
    hd = 2 * DIFF_HEAD_DIM
    vt_rows = (v_dim // hd) * (hd + V_ONES_ROWS)
    n_seq_tiles = seq // tk
    return pl.pallas_call(
        functools.partial(_kv_body, qk_dim=qk_dim),
        grid=(t // tk,),
        in_specs=[pl.BlockSpec((tk, d), lambda i: (i, 0)),
                  _resident((1, d)),
                  _resident((d, qk_dim)),
                  _resident((v_dim, d)),
                  pl.BlockSpec((tk, LANES), lambda i: (i % n_seq_tiles, 0)),
                  pl.BlockSpec((tk, LANES), lambda i: (i % n_seq_tiles, 0))],
        out_specs=[pl.BlockSpec((tk, qk_dim), lambda i: (i, 0)),
                   pl.BlockSpec((1, vt_rows, tk), lambda i: (i, 0, 0))],
        out_shape=[jax.ShapeDtypeStruct((t, qk_dim), BF16),
                   jax.ShapeDtypeStruct((t // tk, vt_rows, tk), BF16)],
        compiler_params=pltpu.CompilerParams(
            dimension_semantics=("arbitrary",), vmem_limit_bytes=VMEM_LIMIT_BYTES),
        name="shared_kv",
    )(x, norm_g.reshape(1, d), w_kv[:, :qk_dim].astype(BF16), w_kv[:, qk_dim:].T.astype(BF16), cos_t, sin_t)


def _attn_body(x_ref, g_ref, wqt_ref, cos_ref, sin_ref, k_ref, vt_ref, lam_ref, sg_ref, wo_ref, o_ref,
               qt_ref, m_ref, acc_ref, os_ref, *, tq, n_heads, lambda_init):
    qi = pl.program_id(1)
    hd = 2 * DIFF_HEAD_DIM
    hv = hd + V_ONES_ROWS
    scale = DIFF_HEAD_DIM ** -0.5 * math.log2(math.e)

    xn = _rms(x_ref[...], g_ref[...]).astype(BF16)
    cos = cos_ref[...]
    sin = sin_ref[...]
    feat = lax.broadcasted_iota(jnp.int32, (hd, tq), 0)
    map0 = feat < DIFF_HEAD_DIM
    for h in range(n_heads):
        qt = _dot_nt(wqt_ref[h * hd:(h + 1) * hd, :], xn)
        qt = _rope_cols(qt, cos, sin) * scale
        qt_ref[h] = jnp.concatenate([jnp.where(map0, qt, 0.0), jnp.where(map0, 0.0, qt)], axis=1).astype(BF16)

    lam = lam_ref[...]
    t1 = jnp.sum(lam[0:1] * lam[1:2], axis=1, keepdims=True)
    t2 = jnp.sum(lam[2:3] * lam[3:4], axis=1, keepdims=True)
    lam_full = jnp.exp(t1) - jnp.exp(t2) + lambda_init

    k_chunk = lax.broadcasted_iota(jnp.int32, (tq, 2 * tq), 0) >> CHUNK_SHIFT
    q_chunk = (lax.broadcasted_iota(jnp.int32, (tq, 2 * tq), 1) & (tq - 1)) >> CHUNK_SHIFT
    diag_mask = k_chunk <= q_chunk

    m_ref[...] = jnp.full(m_ref.shape, -jnp.inf, F32)
    acc_ref[...] = jnp.zeros(acc_ref.shape, F32)

    def kv_block(j, masked):
        rows = pl.ds(pl.multiple_of(j * tq, tq), tq)
        lookahead = 2
        scores = [_dot(k_ref[rows, h * hd:(h + 1) * hd], qt_ref[h]) for h in range(lookahead)]
        for h in range(n_heads):
            if h + lookahead < n_heads:
                nxt = h + lookahead
                scores.append(_dot(k_ref[rows, nxt * hd:(nxt + 1) * hd], qt_ref[nxt]))
            s = scores[h]
            if masked:
                s = jnp.where(diag_mask, s, -jnp.inf)
            m_prev = m_ref[h]
            m_next = jnp.maximum(m_prev, jnp.max(s, axis=0, keepdims=True))
            p = jnp.exp2(s - m_next)
            alpha = jnp.exp2(m_prev - m_next)
            m_ref[h] = m_next
            acc_ref[h] = alpha * acc_ref[h] + _dot(vt_ref[j, h * hv:(h + 1) * hv, :], p.astype(BF16))

    def loop_body(j, carry):
        kv_block(j, False)
        return carry

    lax.fori_loop(0, qi, loop_body, 0)
    kv_block(qi, True)

    sg = jnp.concatenate([sg_ref[...]] * (tq // LANES), axis=1)
    for h in range(n_heads):
        o_all = acc_ref[h, 0:hd, :] * (1.0 / acc_ref[h, hd:hd + 1, :])
        o = o_all[:, 0:tq] - lam_full * o_all[:, tq:2 * tq]
        ms = jnp.mean(o * o, axis=0, keepdims=True)
        o = o * lax.rsqrt(ms + EPS) * sg * (1.0 - lambda_init)
        os_ref[:, h * hd:(h + 1) * hd] = o.T.astype(BF16)

    o_ref[...] = x_ref[...] + _dot(os_ref[...], wo_ref[...])


def _diff_attn(x, norm_g, w_q, cos_tt, sin_tt, k, vt, lam, subln_g, w_o, *, batch, lambda_init, tq):
    t, d = x.shape
    seq = t // batch
    assert seq % tq == 0 and tq % LANES == 0 and (tq & (tq - 1)) == 0 and tq % CHUNK == 0
    qk_dim = w_q.shape[1]
    v_dim = w_o.shape[0]
    hd = 2 * DIFF_HEAD_DIM
    hv = hd + V_ONES_ROWS
    n_heads = qk_dim // hd
    assert vt.shape[1] == n_heads * hv and v_dim == n_heads * hd
    nq = seq // tq
    row_map = lambda b, i: (b * nq + i, 0)
    sg_rows = jnp.broadcast_to(subln_g.reshape(hd, 1), (hd, LANES))
    return pl.pallas_call(
        functools.partial(_attn_body, tq=tq, n_heads=n_heads, lambda_init=lambda_init),
        grid=(batch, nq),
        in_specs=[pl.BlockSpec((tq, d), row_map),
                  _resident((1, d)),
                  _resident((qk_dim, d)),
                  pl.BlockSpec((hd, tq), lambda b, i: (0, i)),
                  pl.BlockSpec((hd, tq), lambda b, i: (0, i)),
                  pl.BlockSpec((seq, qk_dim), lambda b, i: (b, 0)),
                  pl.BlockSpec((nq, n_heads * hv, tq), lambda b, i: (b, 0, 0)),
                  _resident(lam.shape),
                  _resident((hd, LANES)),
                  _resident((v_dim, d))],
        out_specs=pl.BlockSpec((tq, d), row_map),
        out_shape=jax.ShapeDtypeStruct((t, d), F32),
        scratch_shapes=[pltpu.VMEM((n_heads, hd, 2 * tq), BF16),
                        pltpu.VMEM((n_heads, 1, 2 * tq), F32),
                        pltpu.VMEM((n_heads, hv, 2 * tq), F32),
                        pltpu.VMEM((tq, n_heads * hd), BF16)],
        compiler_params=pltpu.CompilerParams(
            dimension_semantics=("arbitrary", "arbitrary"), vmem_limit_bytes=VMEM_LIMIT_BYTES),
        name="diff_attn",
    )(x, norm_g.reshape(1, d), w_q.T.astype(BF16), cos_tt, sin_tt, k, vt, lam, sg_rows, w_o.astype(BF16))


def _rope_tables(seq):
    dim = DIFF_HEAD_DIM
    inv = ROPE_THETA ** (-jnp.arange(0, dim, 2, dtype=F32) / dim)
    ang = jnp.arange(seq, dtype=F32)[:, None] * inv[None, :]
    cos, sin = jnp.cos(ang), jnp.sin(ang)
    cos_t = jnp.tile(cos, (1, 2 * LANES // dim))
    sin_t = jnp.tile(jnp.concatenate([-sin, sin], axis=1), (1, LANES // dim))
    return cos_t, sin_t


def kernel(x, ffn1_norm, ffn1_w_in, ffn1_w_out, mixer_norm, ffn2_norm, ffn2_w_in, ffn2_w_out, ssm_w_in, ssm_conv_w, ssm_conv_b, ssm_dt_bias, ssm_a_log, ssm_d, ssm_norm, ssm_w_out, kv_norm, w_kv, attn_w_q, attn_lambda, attn_subln, attn_w_o, final_norm):
    batch, seq, d = x.shape
    depth = ffn1_norm.shape[0]
    n_a = ssm_w_in.shape[0]
    qk_dim = attn_w_q.shape[2]
    tq = min(256, seq)
    cos_t, sin_t = _rope_tables(seq)
    cos_tt, sin_tt = cos_t.T, sin_t.T
    h = x.reshape(batch * seq, d)
    k = vt = None
    for layer in range(depth):
        h = _ffn(h, ffn1_norm[layer], ffn1_w_in[layer], ffn1_w_out[layer])
        if layer < n_a:
            i = layer
            h = _mamba_mixer(h, mixer_norm[layer], ssm_w_in[i], ssm_conv_w[i], ssm_conv_b[i], ssm_dt_bias[i],
                             ssm_a_log[i], ssm_d[i], ssm_norm[i], ssm_w_out[i], batch=batch)
        else:
            j = layer - n_a
            lambda_init = 0.8 - 0.6 * math.exp(-0.3 * layer)
            h = _diff_attn(h, mixer_norm[layer], attn_w_q[j], cos_tt, sin_tt, k, vt, attn_lambda[j],
                           attn_subln[j], attn_w_o[j], batch=batch, lambda_init=lambda_init, tq=tq)
        last = layer == depth - 1
        h = _ffn(h, ffn2_norm[layer], ffn2_w_in[layer], ffn2_w_out[layer],
                 final_norm if last else None)
        if layer == n_a - 1:
            k, vt = _shared_kv(h, kv_norm, w_kv, cos_t, sin_t, qk_dim=qk_dim, seq=seq, tk=tq)
    return h.reshape(batch, seq, d)
```

```python
import functools
import math

import jax
import jax.numpy as jnp
from jax import lax
from jax.experimental import pallas as pl
from jax.experimental.pallas import tpu as pltpu

F32 = jnp.float32
BF16 = jnp.bfloat16

EPS = 1e-5
CHUNK = 64
CHUNK_SHIFT = CHUNK.bit_length() - 1
SSM_HEAD_DIM = 64
SSM_GROUPS = 8
SSM_STATE = 128
SSM_CONV = 4
SSD_BLOCK = 128
CONV_COL_BLOCK = 512
SCORE_LOOKAHEAD = 2
DIFF_HEAD_DIM = 64
V_ONES_ROWS = 16
ROPE_THETA = 10000.0
LANES = 128
SUBLANES = 8
VMEM_LIMIT_BYTES = 58 * 1024 * 1024


def _resident(shape):
    zeros = (0,) * len(shape)
    return pl.BlockSpec(shape, lambda *_: zeros, pipeline_mode=pl.Buffered(1))


def _rms(x, g):
    ms = jnp.mean(x * x, axis=-1, keepdims=True)
    return x * lax.rsqrt(ms + EPS) * g


def _silu(x):
    return x * jax.nn.sigmoid(x)


def _dot(a, b):
    return jnp.dot(a, b, preferred_element_type=F32)


def _dot_nt(a, b):
    return lax.dot_general(a, b, (((1,), (1,)), ((), ())), preferred_element_type=F32)


def _dot_tn(a, b):
    return lax.dot_general(a, b, (((0,), (0,)), ((), ())), preferred_element_type=F32)


def _ffn_body(*refs, d_ff, tf, final_norm):
    if final_norm:
        x_ref, g_ref, win_ref, wout_ref, fg_ref, o_ref, xn_ref = refs
    else:
        x_ref, g_ref, win_ref, wout_ref, o_ref, xn_ref = refs
    x = x_ref[...]
    xn_ref[...] = _rms(x, g_ref[...]).astype(BF16)
    o_ref[...] = x
    for j in range(d_ff // tf):
        xn = xn_ref[...]
        gate = _dot(xn, win_ref[:, j * tf:(j + 1) * tf])
        up = _dot(xn, win_ref[:, d_ff + j * tf:d_ff + (j + 1) * tf])
        h = (_silu(gate) * up * 0.5).astype(BF16)
        o_ref[...] += _dot(h, wout_ref[j * tf:(j + 1) * tf, :])
    if final_norm:
        o_ref[...] = _rms(o_ref[...], fg_ref[...])


def _ffn(x, norm_g, w_in, w_out, final_g=None, *, tm=512, tf=256):
    t, d = x.shape
    d_ff = w_out.shape[0]
    tm = min(tm, t)
    assert d_ff % tf == 0 and t % tm == 0
    final_norm = final_g is not None
    operands = [x, norm_g.reshape(1, d), w_in.astype(BF16), w_out.astype(BF16)]
    in_specs = [pl.BlockSpec((tm, d), lambda i: (i, 0)),
                _resident((1, d)),
                _resident((d, 2 * d_ff)),
                _resident((d_ff, d))]
    if final_norm:
        operands.append(final_g.reshape(1, d))
        in_specs.append(_resident((1, d)))
    return pl.pallas_call(
        functools.partial(_ffn_body, d_ff=d_ff, tf=tf, final_norm=final_norm),
        grid=(t // tm,),
        in_specs=in_specs,
        out_specs=pl.BlockSpec((tm, d), lambda i: (i, 0)),
        out_shape=jax.ShapeDtypeStruct((t, d), F32),
        scratch_shapes=[pltpu.VMEM((tm, d), BF16)],
        compiler_params=pltpu.CompilerParams(
            dimension_semantics=("arbitrary",), vmem_limit_bytes=VMEM_LIMIT_BYTES),
        name="ffn_final" if final_norm else "ffn",
    )(*operands)


def _expand_heads(v, e2_ref):
    hi = v.astype(BF16)
    lo = (v - hi.astype(F32)).astype(BF16)
    return _dot(jnp.concatenate([hi, lo], axis=1), e2_ref[...])


def _mamba_body(x_ref, g_ref, win_ref, cw_ref, cb_ref, dtb_ref, alog_ref, dsk_ref, ng_ref, e2_ref,
                wout_ref, o_ref,
                xn_ref, ext_ref, xs_ref, xsb_ref, bc_ref, y_ref, xdtd_ref, expa_ref, fac_ref, acum_ref, pt_ref,
                state_ref, *, ts, d_inner, n_groups, n_state, n_heads):
    s_idx = pl.program_id(1)
    gw = d_inner // n_groups
    hpg = gw // SSM_HEAD_DIM
    bc_dim = n_groups * n_state
    conv_dim = d_inner + 2 * bc_dim
    blk = SSD_BLOCK
    n_blocks = ts // blk

    xn_ref[...] = _rms(x_ref[...], g_ref[...]).astype(BF16)

    base = 2 * SUBLANES
    @pl.when(s_idx == 0)
    def _():
        ext_ref[:, SUBLANES:base, :] = jnp.zeros((ext_ref.shape[0], SUBLANES, LANES), F32)
        state_ref[...] = jnp.zeros(state_ref.shape, F32)

    @pl.when(s_idx > 0)
    def _():
        ext_ref[:, SUBLANES:base, :] = ext_ref[:, 2 * ts + SUBLANES:2 * ts + base, :]

    li = lax.broadcasted_iota(jnp.int32, (blk, blk), 0)
    si = lax.broadcasted_iota(jnp.int32, (blk, blk), 1)
    causal = si <= li
    tri = jnp.where(causal, 1.0, 0.0).astype(F32)
    head_of_lane = lax.broadcasted_iota(jnp.int32, (blk, gw), 1) >> (SSM_HEAD_DIM.bit_length() - 1)
    lane_h = lax.broadcasted_iota(jnp.int32, (blk, LANES), 1)
    a_neg = -jnp.exp(alog_ref[...])
    cblk = CONV_COL_BLOCK
    n_chunks = conv_dim // cblk

    def block_rows(c):
        return slice(c * blk, (c + 1) * blk)

    def decay_terms(c):
        rows = block_rows(c)
        dtv = _dot(xn_ref[rows, :], win_ref[:, d_inner + conv_dim:d_inner + conv_dim + LANES]) + dtb_ref[...]
        dtv = jnp.maximum(dtv, 0.0) + jnp.log1p(jnp.exp(-jnp.abs(dtv)))
        a_cum = jnp.dot(tri, dtv * a_neg, precision=lax.Precision.HIGHEST, preferred_element_type=F32)
        acum_ref[rows, :] = a_cum
        expa_ref[rows, :] = _expand_heads(jnp.exp(a_cum), e2_ref)
        fac_ref[rows, :] = _expand_heads(dtv * jnp.exp(a_cum[blk - 1:blk, :] - a_cum), e2_ref)
        pt_ref[:, rows] = jnp.where(lane_h < n_heads, a_cum, pltpu.roll(dtv, n_heads, 1)).T

    def inproj_conv(j):
        rows = slice(0, ts)
        proj = _dot(xn_ref[...], win_ref[:, d_inner + j * cblk:d_inner + (j + 1) * cblk])
        for q in range(cblk // LANES):
            slab = j * (cblk // LANES) + q
            cols = slice(slab * LANES, (slab + 1) * LANES)
            ext_ref[slab, pl.ds(base, ts, stride=2), :] = proj[:, q * LANES:(q + 1) * LANES]
            acc = cb_ref[:, cols]
            for k in range(SSM_CONV):
                back = SSM_CONV - 1 - k
                acc = acc + cw_ref[k:k + 1, cols] * ext_ref[slab, pl.ds(base - 2 * back, ts, stride=2), :]
            a = _silu(acc)
            if slab * LANES < d_inner:
                xs_ref[rows, cols] = a
                xsb_ref[rows, cols] = a.astype(BF16)
                xdtd_ref[rows, cols] = (a * fac_ref[rows, cols]).astype(BF16)
            else:
                bc_ref[rows, slab * LANES - d_inner:(slab + 1) * LANES - d_inner] = a.astype(BF16)

    def bc_group(c, g):
        rows = block_rows(c)
        b_g = bc_ref[rows, g * n_state:(g + 1) * n_state]
        c_g = bc_ref[rows, bc_dim + g * n_state:bc_dim + (g + 1) * n_state]
        return b_g, c_g

    def scan_scores(c, g):
        b_g, c_g = bc_group(c, g)
        return _dot_nt(c_g, b_g)

    def scan_state(c, g):
        rows = block_rows(c)
        gcols = slice(g * gw, (g + 1) * gw)
        b_g, c_g = bc_group(c, g)
        prev = state_ref[g]
        y_ref[rows, gcols] = _dot(c_g, prev.astype(BF16)) * expa_ref[rows, gcols]
        upd = _dot_tn(b_g, xdtd_ref[rows, gcols])
        state_ref[g] = prev * expa_ref[(c + 1) * blk - 1:(c + 1) * blk, gcols] + upd

    def scan_intra(c, g, cb):
        rows = block_rows(c)
        gcols = slice(g * gw, (g + 1) * gw)
        xg = xsb_ref[rows, gcols]
        m_parts, x_parts = [], []
        for h4 in range(hpg):
            hh = g * hpg + h4
            seg = acum_ref[rows, hh:hh + 1] - pt_ref[hh:hh + 1, rows]
            decay = jnp.exp(jnp.where(causal, seg, -jnp.inf))
            m_parts.append((cb * pt_ref[n_heads + hh:n_heads + hh + 1, rows] * decay).astype(BF16))
            x_parts.append(jnp.where(head_of_lane == h4, xg, jnp.zeros_like(xg)))
        y_ref[rows, gcols] += _dot(jnp.concatenate(m_parts, axis=1), jnp.concatenate(x_parts, axis=0))

    def gate_group(g):
        rows = slice(0, ts)
        gcols = slice(g * gw, (g + 1) * gw)
        z = _dot(xn_ref[rows, :], win_ref[:, gcols])
        y = y_ref[rows, gcols] + dsk_ref[:, gcols] * xs_ref[rows, gcols]
        yg = y * _silu(z)
        ms = jnp.mean(yg * yg, axis=-1, keepdims=True)
        xsb_ref[rows, gcols] = (yg * lax.rsqrt(ms + EPS) * ng_ref[:, gcols]).astype(BF16)

    for c in range(n_blocks):
        decay_terms(c)
    for j in range(n_chunks):
        inproj_conv(j)
    steps = [(c, g) for c in range(n_blocks) for g in range(n_groups)]
    cb_next = scan_scores(*steps[0])
    for i, (c, g) in enumerate(steps):
        cb = cb_next
        if i + 1 < len(steps):
            cb_next = scan_scores(*steps[i + 1])
        scan_state(c, g)
        scan_intra(c, g, cb)
    for g in range(n_groups):
        gate_group(g)
    o_ref[...] = x_ref[...] + _dot(xsb_ref[...], wout_ref[...])


def _mamba_mixer(x, norm_g, w_in, conv_w, conv_b, dt_bias, a_log, d_skip, ssm_norm_g, w_out, *, batch, ts=256):
    t, d = x.shape
    seq = t // batch
    ts = min(ts, seq)
    d_inner = w_out.shape[0]
    conv_dim = conv_w.shape[1]
    heads = dt_bias.shape[0]
    n_groups, n_state = SSM_GROUPS, SSM_STATE
    assert seq % ts == 0 and ts % SSD_BLOCK == 0 and 2 * heads <= LANES
    assert conv_dim == d_inner + 2 * n_groups * n_state and heads * SSM_HEAD_DIM == d_inner
    ns = seq // ts
    pad_h = LANES - heads
    n_pad = d_inner + conv_dim + LANES
    e_mat = (jnp.arange(LANES)[:, None] == (jnp.arange(d_inner)[None, :] // SSM_HEAD_DIM)).astype(BF16)
    operands = [
        x, norm_g.reshape(1, d),
        jnp.pad(w_in.astype(BF16), ((0, 0), (0, pad_h))),
        conv_w, conv_b.reshape(1, conv_dim),
        jnp.pad(dt_bias, (0, pad_h)).reshape(1, LANES),
        jnp.pad(a_log, (0, pad_h)).reshape(1, LANES),
        jnp.repeat(d_skip, SSM_HEAD_DIM).reshape(1, d_inner),
        ssm_norm_g.reshape(1, d_inner),
        jnp.concatenate([e_mat, e_mat], axis=0),
        w_out.astype(BF16),
    ]
    row_map = lambda b, s: (b * ns + s, 0)
    in_specs = [pl.BlockSpec((ts, d), row_map),
                _resident((1, d)),
                _resident((d, n_pad)),
                _resident((SSM_CONV, conv_dim)),
                _resident((1, conv_dim)),
                _resident((1, LANES)),
                _resident((1, LANES)),
                _resident((1, d_inner)),
                _resident((1, d_inner)),
                _resident((2 * LANES, d_inner)),
                _resident((d_inner, d))]
    return pl.pallas_call(
        functools.partial(_mamba_body, ts=ts, d_inner=d_inner, n_groups=n_groups, n_state=n_state,
                          n_heads=heads),
        grid=(batch, ns),
        in_specs=in_specs,
        out_specs=pl.BlockSpec((ts, d), row_map),
        out_shape=jax.ShapeDtypeStruct((t, d), F32),
        scratch_shapes=[pltpu.VMEM((ts, d), BF16),
                        pltpu.VMEM((conv_dim // LANES, 2 * (ts + SUBLANES), LANES), F32),
                        pltpu.VMEM((ts, d_inner), F32),
                        pltpu.VMEM((ts, d_inner), BF16),
                        pltpu.VMEM((ts, 2 * n_groups * n_state), BF16),
                        pltpu.VMEM((ts, d_inner), F32),
                        pltpu.VMEM((ts, d_inner), BF16),
                        pltpu.VMEM((ts, d_inner), F32),
                        pltpu.VMEM((ts, d_inner), F32),
                        pltpu.VMEM((ts, LANES), F32),
                        pltpu.VMEM((LANES, ts), F32),
                        pltpu.VMEM((n_groups, n_state, d_inner // n_groups), F32)],
        compiler_params=pltpu.CompilerParams(
            dimension_semantics=("arbitrary", "arbitrary"), vmem_limit_bytes=VMEM_LIMIT_BYTES),
        name="mamba_mixer",
    )(*operands)


def _rope_rows(xb, cos, sin_signed):
    lane = lax.broadcasted_iota(jnp.int32, xb.shape, 1)
    first_half = (lane & (DIFF_HEAD_DIM - 1)) < (DIFF_HEAD_DIM // 2)
    rot = jnp.where(first_half,
                    pltpu.roll(xb, LANES - DIFF_HEAD_DIM // 2, 1),
                    pltpu.roll(xb, DIFF_HEAD_DIM // 2, 1))
    return xb * cos + rot * sin_signed


def _rope_cols(xt, cos_t, sin_signed_t):
    half = DIFF_HEAD_DIM // 2
    parts = []
    for u in range(xt.shape[0] // DIFF_HEAD_DIM):
        base = u * DIFF_HEAD_DIM
        parts += [xt[base + half:base + 2 * half], xt[base:base + half]]
    return xt * cos_t + jnp.concatenate(parts, axis=0) * sin_signed_t


def _kv_body(x_ref, g_ref, wk_ref, wvt_ref, cos_ref, sin_ref, k_ref, vt_ref, *, qk_dim):
    xn = _rms(x_ref[...], g_ref[...]).astype(BF16)
    cos = cos_ref[...]
    sin = sin_ref[...]
    for j in range(qk_dim // LANES):
        kb = _dot(xn, wk_ref[:, j * LANES:(j + 1) * LANES])
        k_ref[:, j * LANES:(j + 1) * LANES] = _rope_rows(kb, cos, sin).astype(BF16)
    hd = 2 * DIFF_HEAD_DIM
    ones = jnp.ones((V_ONES_ROWS, xn.shape[0]), BF16)
    vt = _dot_nt(wvt_ref[...], xn).astype(BF16)
    for h in range(wvt_ref.shape[0] // hd):
        base = h * (hd + V_ONES_ROWS)
        vt_ref[0, base:base + hd, :] = vt[h * hd:(h + 1) * hd]
        vt_ref[0, base + hd:base + hd + V_ONES_ROWS, :] = ones


def _shared_kv(x, norm_g, w_kv, cos_t, sin_t, *, qk_dim, seq, tk):
    t, d = x.shape
    v_dim = w_kv.shape[1] - qk_dim
    hd = 2 * DIFF_HEAD_DIM
    vt_rows = (v_dim // hd) * (hd + V_ONES_ROWS)
    n_seq_tiles = seq // tk
    return pl.pallas_call(
        functools.partial(_kv_body, qk_dim=qk_dim),
        grid=(t // tk,),
        in_specs=[pl.BlockSpec((tk, d), lambda i: (i, 0)),
                  _resident((1, d)),
                  _resident((d, qk_dim)),
                  _resident((v_dim, d)),
                  pl.BlockSpec((tk, LANES), lambda i: (i % n_seq_tiles, 0)),
                  pl.BlockSpec((tk, LANES), lambda i: (i % n_seq_tiles, 0))],
        out_specs=[pl.BlockSpec((tk, qk_dim), lambda i: (i, 0)),
                   pl.BlockSpec((1, vt_rows, tk), lambda i: (i, 0, 0))],
        out_shape=[jax.ShapeDtypeStruct((t, qk_dim), BF16),
                   jax.ShapeDtypeStruct((t // tk, vt_rows, tk), BF16)],
        compiler_params=pltpu.CompilerParams(
            dimension_semantics=("arbitrary",), vmem_limit_bytes=VMEM_LIMIT_BYTES),
        name="shared_kv",
    )(x, norm_g.reshape(1, d), w_kv[:, :qk_dim].astype(BF16), w_kv[:, qk_dim:].T.astype(BF16), cos_t, sin_t)


def _attn_body(x_ref, g_ref, wqt_ref, cos_ref, sin_ref, k_ref, vt_ref, lam_ref, sg_ref, wo_ref, o_ref,
               qt_ref, sc_ref, m_ref, acc_ref, os_ref, *, tq, n_heads, lambda_init):
    qi = pl.program_id(1)
    hd = 2 * DIFF_HEAD_DIM
    hv = hd + V_ONES_ROWS
    scale = DIFF_HEAD_DIM ** -0.5 * math.log2(math.e)

    xn = _rms(x_ref[...], g_ref[...]).astype(BF16)
    cos = cos_ref[...]
    sin = sin_ref[...]
    feat = lax.broadcasted_iota(jnp.int32, (hd, tq), 0)
    map0 = feat < DIFF_HEAD_DIM
    qt_all = _dot_nt(wqt_ref[...], xn)
    for h in range(n_heads):
        qt = _rope_cols(qt_all[h * hd:(h + 1) * hd], cos, sin) * scale
        qt_ref[h] = jnp.concatenate([jnp.where(map0, qt, 0.0), jnp.where(map0, 0.0, qt)], axis=1).astype(BF16)

    lam = lam_ref[...]
    t1 = jnp.sum(lam[0:1] * lam[1:2], axis=1, keepdims=True)
    t2 = jnp.sum(lam[2:3] * lam[3:4], axis=1, keepdims=True)
    lam_full = jnp.exp(t1) - jnp.exp(t2) + lambda_init

    k_chunk = lax.broadcasted_iota(jnp.int32, (tq, 2 * tq), 0) >> CHUNK_SHIFT
    q_chunk = (lax.broadcasted_iota(jnp.int32, (tq, 2 * tq), 1) & (tq - 1)) >> CHUNK_SHIFT
    diag_mask = k_chunk <= q_chunk

    m_ref[...] = jnp.full(m_ref.shape, -jnp.inf, F32)
    acc_ref[...] = jnp.zeros(acc_ref.shape, F32)

    def score(j, h):
        rows = pl.ds(pl.multiple_of(j * tq, tq), tq)
        return _dot(k_ref[rows, h * hd:(h + 1) * hd], qt_ref[h])

    for h in range(SCORE_LOOKAHEAD):
        sc_ref[h] = score(0, h)

    def kv_block(j, masked):
        scores = {}
        for h in range(n_heads):
            nxt = h + SCORE_LOOKAHEAD
            if nxt < n_heads:
                scores[nxt] = score(j, nxt)
            elif not masked:
                sc_ref[nxt - n_heads] = score(j + 1, nxt - n_heads)
            s = sc_ref[h] if h < SCORE_LOOKAHEAD else scores.pop(h)
            if masked:
                s = jnp.where(diag_mask, s, -jnp.inf)
            m_prev = m_ref[h]
            m_next = jnp.maximum(m_prev, jnp.max(s, axis=0, keepdims=True))
            p = jnp.exp2(s - m_next)
            alpha = jnp.exp2(m_prev - m_next)
            m_ref[h] = m_next
            acc_ref[h] = alpha * acc_ref[h] + _dot(vt_ref[j, h * hv:(h + 1) * hv, :], p.astype(BF16))

    def loop_body(j, carry):
        kv_block(j, False)
        return carry

    lax.fori_loop(0, qi, loop_body, 0)
    kv_block(qi, True)

    sg = jnp.concatenate([sg_ref[...]] * (tq // LANES), axis=1)
    for h in range(n_heads):
        o_all = acc_ref[h, 0:hd, :] * (1.0 / acc_ref[h, hd:hd + 1, :])
        o = o_all[:, 0:tq] - lam_full * o_all[:, tq:2 * tq]
        ms = jnp.mean(o * o, axis=0, keepdims=True)
        o = o * lax.rsqrt(ms + EPS) * sg * (1.0 - lambda_init)
        os_ref[:, h * hd:(h + 1) * hd] = o.T.astype(BF16)

    o_ref[...] = x_ref[...] + _dot(os_ref[...], wo_ref[...])


def _diff_attn(x, norm_g, w_q, cos_tt, sin_tt, k, vt, lam, subln_g, w_o, *, batch, lambda_init, tq):
    t, d = x.shape
    seq = t // batch
    assert seq % tq == 0 and tq % LANES == 0 and (tq & (tq - 1)) == 0 and tq % CHUNK == 0
    qk_dim = w_q.shape[1]
    v_dim = w_o.shape[0]
    hd = 2 * DIFF_HEAD_DIM
    hv = hd + V_ONES_ROWS
    n_heads = qk_dim // hd
    assert vt.shape[1] == n_heads * hv and v_dim == n_heads * hd
    nq = seq // tq
    row_map = lambda b, i: (b * nq + i, 0)
    sg_rows = jnp.broadcast_to(subln_g.reshape(hd, 1), (hd, LANES))
    return pl.pallas_call(
        functools.partial(_attn_body, tq=tq, n_heads=n_heads, lambda_init=lambda_init),
        grid=(batch, nq),
        in_specs=[pl.BlockSpec((tq, d), row_map),
                  _resident((1, d)),
                  _resident((qk_dim, d)),
                  pl.BlockSpec((hd, tq), lambda b, i: (0, i)),
                  pl.BlockSpec((hd, tq), lambda b, i: (0, i)),
                  pl.BlockSpec((seq, qk_dim), lambda b, i: (b, 0)),
                  pl.BlockSpec((nq, n_heads * hv, tq), lambda b, i: (b, 0, 0)),
                  _resident(lam.shape),
                  _resident((hd, LANES)),
                  _resident((v_dim, d))],
        out_specs=pl.BlockSpec((tq, d), row_map),
        out_shape=jax.ShapeDtypeStruct((t, d), F32),
        scratch_shapes=[pltpu.VMEM((n_heads, hd, 2 * tq), BF16),
                        pltpu.VMEM((SCORE_LOOKAHEAD, tq, 2 * tq), F32),
                        pltpu.VMEM((n_heads, 1, 2 * tq), F32),
                        pltpu.VMEM((n_heads, hv, 2 * tq), F32),
                        pltpu.VMEM((tq, n_heads * hd), BF16)],
        compiler_params=pltpu.CompilerParams(
            dimension_semantics=("arbitrary", "arbitrary"), vmem_limit_bytes=VMEM_LIMIT_BYTES),
        name="diff_attn",
    )(x, norm_g.reshape(1, d), w_q.T.astype(BF16), cos_tt, sin_tt, k, vt, lam, sg_rows, w_o.astype(BF16))


def _rope_tables(seq):
    dim = DIFF_HEAD_DIM
    inv = ROPE_THETA ** (-jnp.arange(0, dim, 2, dtype=F32) / dim)
    ang = jnp.arange(seq, dtype=F32)[:, None] * inv[None, :]
    cos, sin = jnp.cos(ang), jnp.sin(ang)
    cos_t = jnp.tile(cos, (1, 2 * LANES // dim))
    sin_t = jnp.tile(jnp.concatenate([-sin, sin], axis=1), (1, LANES // dim))
    return cos_t, sin_t


def kernel(x, ffn1_norm, ffn1_w_in, ffn1_w_out, mixer_norm, ffn2_norm, ffn2_w_in, ffn2_w_out, ssm_w_in, ssm_conv_w, ssm_conv_b, ssm_dt_bias, ssm_a_log, ssm_d, ssm_norm, ssm_w_out, kv_norm, w_kv, attn_w_q, attn_lambda, attn_subln, attn_w_o, final_norm):
    batch, seq, d = x.shape
    depth = ffn1_norm.shape[0]
    n_a = ssm_w_in.shape[0]
    qk_dim = attn_w_q.shape[2]
    tq = min(256, seq)
    cos_t, sin_t = _rope_tables(seq)
    cos_tt, sin_tt = cos_t.T, sin_t.T
    h = x.reshape(batch * seq, d)
    k = vt = None
    for layer in range(depth):
        h = _ffn(h, ffn1_norm[layer], ffn1_w_in[layer], ffn1_w_out[layer])
        if layer < n_a:
            i = layer
            h = _mamba_mixer(h, mixer_norm[layer], ssm_w_in[i], ssm_conv_w[i], ssm_conv_b[i], ssm_dt_bias[i],
                             ssm_a_log[i], ssm_d[i], ssm_norm[i], ssm_w_out[i], batch=batch)
        else:
            j = layer - n_a
            lambda_init = 0.8 - 0.6 * math.exp(-0.3 * layer)
            h = _diff_attn(h, mixer_norm[layer], attn_w_q[j], cos_tt, sin_tt, k, vt, attn_lambda[j],
                           attn_subln[j], attn_w_o[j], batch=batch, lambda_init=lambda_init, tq=tq)
        last = layer == depth - 1
        h = _ffn(h, ffn2_norm[layer], ffn2_w_in[layer], ffn2_w_out[layer],
                 final_norm if last else None)
        if layer == n_a - 1:
            k, vt = _shared_kv(h, kv_norm, w_kv, cos_t, sin_t, qk_dim=qk_dim, seq=seq, tk=tq)
    return h.reshape(batch, seq, d)
```

```python
import functools
import math

import jax
import jax.numpy as jnp
from jax import lax
from jax.experimental import pallas as pl
from jax.experimental.pallas import tpu as pltpu

F32 = jnp.float32
BF16 = jnp.bfloat16

EPS = 1e-5
CHUNK = 64
CHUNK_SHIFT = CHUNK.bit_length() - 1
SSM_HEAD_DIM = 64
SSM_GROUPS = 8
SSM_STATE = 128
SSM_CONV = 4
SSD_BLOCK = 128
CONV_COL_BLOCK = 512
SCORE_LOOKAHEAD = 2
DIFF_HEAD_DIM = 64
V_ONES_ROWS = 16
ROPE_THETA = 10000.0
LANES = 128
SUBLANES = 8
VMEM_LIMIT_BYTES = 58 * 1024 * 1024


def _resident(shape):
    zeros = (0,) * len(shape)
    return pl.BlockSpec(shape, lambda *_: zeros, pipeline_mode=pl.Buffered(1))


def _rms(x, g):
    ms = jnp.mean(x * x, axis=-1, keepdims=True)
    return x * lax.rsqrt(ms + EPS) * g


def _silu(x):
    return x * jax.nn.sigmoid(x)


def _dot(a, b):
    return jnp.dot(a, b, preferred_element_type=F32)


def _dot_nt(a, b):
    return lax.dot_general(a, b, (((1,), (1,)), ((), ())), preferred_element_type=F32)


def _dot_tn(a, b):
    return lax.dot_general(a, b, (((0,), (0,)), ((), ())), preferred_element_type=F32)


def _ffn_body(*refs, d_ff, tf, final_norm):
    if final_norm:
        x_ref, g_ref, win_ref, wout_ref, fg_ref, o_ref, xn_ref = refs
    else:
        x_ref, g_ref, win_ref, wout_ref, o_ref, xn_ref = refs
    x = x_ref[...]
    xn_ref[...] = _rms(x, g_ref[...]).astype(BF16)
    o_ref[...] = x
    for j in range(d_ff // tf):
        xn = xn_ref[...]
        gate = _dot(xn, win_ref[:, j * tf:(j + 1) * tf])
        up = _dot(xn, win_ref[:, d_ff + j * tf:d_ff + (j + 1) * tf])
        h = (_silu(gate) * up * 0.5).astype(BF16)
        o_ref[...] += _dot(h, wout_ref[j * tf:(j + 1) * tf, :])
    if final_norm:
        o_ref[...] = _rms(o_ref[...], fg_ref[...])


def _ffn(x, norm_g, w_in_all, w_out_all, layer, final_g=None, *, tm=1024, tf=256):
    t, d = x.shape
    d_ff = w_out_all.shape[1]
    tm = min(tm, t)
    assert d_ff % tf == 0 and t % tm == 0
    final_norm = final_g is not None
    operands = [x, norm_g.reshape(1, d), w_in_all, w_out_all]
    in_specs = [pl.BlockSpec((tm, d), lambda i: (i, 0)),
                _resident((1, d)),
                pl.BlockSpec((None, d, 2 * d_ff), lambda i: (layer, 0, 0), pipeline_mode=pl.Buffered(1)),
                pl.BlockSpec((None, d_ff, d), lambda i: (layer, 0, 0), pipeline_mode=pl.Buffered(1))]
    if final_norm:
        operands.append(final_g.reshape(1, d))
        in_specs.append(_resident((1, d)))
    return pl.pallas_call(
        functools.partial(_ffn_body, d_ff=d_ff, tf=tf, final_norm=final_norm),
        grid=(t // tm,),
        in_specs=in_specs,
        out_specs=pl.BlockSpec((tm, d), lambda i: (i, 0)),
        out_shape=jax.ShapeDtypeStruct((t, d), F32),
        scratch_shapes=[pltpu.VMEM((tm, d), BF16)],
        compiler_params=pltpu.CompilerParams(
            dimension_semantics=("arbitrary",), vmem_limit_bytes=VMEM_LIMIT_BYTES),
        name="ffn_final" if final_norm else "ffn",
    )(*operands)


def _expand_heads(v, e2_ref):
    hi = v.astype(BF16)
    lo = (v - hi.astype(F32)).astype(BF16)
    return _dot(jnp.concatenate([hi, lo], axis=1), e2_ref[...])


def _mamba_body(x_ref, g_ref, win_ref, cw_ref, cb_ref, dtb_ref, alog_ref, dsk_ref, ng_ref, e2_ref,
                wout_ref, o_ref,
                xn_ref, ext_ref, xs_ref, xsb_ref, bc_ref, y_ref, xdtd_ref, expa_ref, fac_ref, acum_ref, pt_ref,
                state_ref, *, ts, d_inner, n_groups, n_state, n_heads):
    s_idx = pl.program_id(1)
    gw = d_inner // n_groups
    hpg = gw // SSM_HEAD_DIM
    bc_dim = n_groups * n_state
    conv_dim = d_inner + 2 * bc_dim
    blk = SSD_BLOCK
    n_blocks = ts // blk

    xn_ref[...] = _rms(x_ref[...], g_ref[...]).astype(BF16)

    base = 2 * SUBLANES
    @pl.when(s_idx == 0)
    def _():
        ext_ref[:, SUBLANES:base, :] = jnp.zeros((ext_ref.shape[0], SUBLANES, LANES), F32)
        state_ref[...] = jnp.zeros(state_ref.shape, F32)

    @pl.when(s_idx > 0)
    def _():
        ext_ref[:, SUBLANES:base, :] = ext_ref[:, 2 * ts + SUBLANES:2 * ts + base, :]

    li = lax.broadcasted_iota(jnp.int32, (blk, blk), 0)
    si = lax.broadcasted_iota(jnp.int32, (blk, blk), 1)
    causal = si <= li
    tri = jnp.where(causal, 1.0, 0.0).astype(F32)
    head_of_lane = lax.broadcasted_iota(jnp.int32, (blk, gw), 1) >> (SSM_HEAD_DIM.bit_length() - 1)
    lane_h = lax.broadcasted_iota(jnp.int32, (blk, LANES), 1)
    a_neg = -jnp.exp(alog_ref[...])
    cblk = CONV_COL_BLOCK
    n_chunks = conv_dim // cblk

    def block_rows(c):
        return slice(c * blk, (c + 1) * blk)

    def decay_terms(c):
        rows = block_rows(c)
        dtv = _dot(xn_ref[rows, :], win_ref[:, d_inner + conv_dim:d_inner + conv_dim + LANES]) + dtb_ref[...]
        dtv = jnp.maximum(dtv, 0.0) + jnp.log1p(jnp.exp(-jnp.abs(dtv)))
        a_cum = jnp.dot(tri, dtv * a_neg, precision=lax.Precision.HIGHEST, preferred_element_type=F32)
        acum_ref[rows, :] = a_cum
        expa_ref[rows, :] = _expand_heads(jnp.exp(a_cum), e2_ref)
        fac_ref[rows, :] = _expand_heads(dtv * jnp.exp(a_cum[blk - 1:blk, :] - a_cum), e2_ref)
        pt_ref[:, rows] = jnp.where(lane_h < n_heads, a_cum, pltpu.roll(dtv, n_heads, 1)).T

    def inproj_conv(j):
        rows = slice(0, ts)
        proj = _dot(xn_ref[...], win_ref[:, d_inner + j * cblk:d_inner + (j + 1) * cblk])
        for q in range(cblk // LANES):
            slab = j * (cblk // LANES) + q
            cols = slice(slab * LANES, (slab + 1) * LANES)
            ext_ref[slab, pl.ds(base, ts, stride=2), :] = proj[:, q * LANES:(q + 1) * LANES]
            acc = cb_ref[:, cols]
            for k in range(SSM_CONV):
                back = SSM_CONV - 1 - k
                acc = acc + cw_ref[k:k + 1, cols] * ext_ref[slab, pl.ds(base - 2 * back, ts, stride=2), :]
            a = _silu(acc)
            if slab * LANES < d_inner:
                xs_ref[rows, cols] = a
                xsb_ref[rows, cols] = a.astype(BF16)
                xdtd_ref[rows, cols] = (a * fac_ref[rows, cols]).astype(BF16)
            else:
                bc_ref[rows, slab * LANES - d_inner:(slab + 1) * LANES - d_inner] = a.astype(BF16)

    def bc_group(c, g):
        rows = block_rows(c)
        b_g = bc_ref[rows, g * n_state:(g + 1) * n_state]
        c_g = bc_ref[rows, bc_dim + g * n_state:bc_dim + (g + 1) * n_state]
        return b_g, c_g

    def scan_scores(c, g):
        b_g, c_g = bc_group(c, g)
        return _dot_nt(c_g, b_g)

    def scan_state(c, g):
        rows = block_rows(c)
        gcols = slice(g * gw, (g + 1) * gw)
        b_g, c_g = bc_group(c, g)
        prev = state_ref[g]
        y_ref[rows, gcols] = _dot(c_g, prev.astype(BF16)) * expa_ref[rows, gcols]
        upd = _dot_tn(b_g, xdtd_ref[rows, gcols])
        state_ref[g] = prev * expa_ref[(c + 1) * blk - 1:(c + 1) * blk, gcols] + upd

    def scan_intra(c, g, cb):
        rows = block_rows(c)
        gcols = slice(g * gw, (g + 1) * gw)
        xg = xsb_ref[rows, gcols]
        m_parts, x_parts = [], []
        for h4 in range(hpg):
            hh = g * hpg + h4
            seg = acum_ref[rows, hh:hh + 1] - pt_ref[hh:hh + 1, rows]
            decay = jnp.exp(jnp.where(causal, seg, -jnp.inf))
            m_parts.append((cb * pt_ref[n_heads + hh:n_heads + hh + 1, rows] * decay).astype(BF16))
            x_parts.append(jnp.where(head_of_lane == h4, xg, jnp.zeros_like(xg)))
        y_ref[rows, gcols] += _dot(jnp.concatenate(m_parts, axis=1), jnp.concatenate(x_parts, axis=0))

    def gate_group(g):
        rows = slice(0, ts)
        gcols = slice(g * gw, (g + 1) * gw)
        z = _dot(xn_ref[rows, :], win_ref[:, gcols])
        y = y_ref[rows, gcols] + dsk_ref[:, gcols] * xs_ref[rows, gcols]
        yg = y * _silu(z)
        ms = jnp.mean(yg * yg, axis=-1, keepdims=True)
        xsb_ref[rows, gcols] = (yg * lax.rsqrt(ms + EPS) * ng_ref[:, gcols]).astype(BF16)

    for c in range(n_blocks):
        decay_terms(c)
    for j in range(n_chunks):
        inproj_conv(j)
    steps = [(c, g) for c in range(n_blocks) for g in range(n_groups)]
    cb_next = scan_scores(*steps[0])
    for i, (c, g) in enumerate(steps):
        cb = cb_next
        if i + 1 < len(steps):
            cb_next = scan_scores(*steps[i + 1])
        scan_state(c, g)
        scan_intra(c, g, cb)
    for g in range(n_groups):
        gate_group(g)
    o_ref[...] = x_ref[...] + _dot(xsb_ref[...], wout_ref[...])


def _mamba_mixer(x, norm_g, w_in, conv_w, conv_b, dt_bias, a_log, d_skip, ssm_norm_g, w_out, *, batch, ts=256):
    t, d = x.shape
    seq = t // batch
    ts = min(ts, seq)
    d_inner = w_out.shape[0]
    conv_dim = conv_w.shape[1]
    heads = dt_bias.shape[0]
    n_groups, n_state = SSM_GROUPS, SSM_STATE
    assert seq % ts == 0 and ts % SSD_BLOCK == 0 and 2 * heads <= LANES
    assert conv_dim == d_inner + 2 * n_groups * n_state and heads * SSM_HEAD_DIM == d_inner
    ns = seq // ts
    pad_h = LANES - heads
    n_pad = d_inner + conv_dim + LANES
    e_mat = (jnp.arange(LANES)[:, None] == (jnp.arange(d_inner)[None, :] // SSM_HEAD_DIM)).astype(BF16)
    operands = [
        x, norm_g.reshape(1, d),
        jnp.pad(w_in.astype(BF16), ((0, 0), (0, pad_h))),
        conv_w, conv_b.reshape(1, conv_dim),
        jnp.pad(dt_bias, (0, pad_h)).reshape(1, LANES),
        jnp.pad(a_log, (0, pad_h)).reshape(1, LANES),
        jnp.repeat(d_skip, SSM_HEAD_DIM).reshape(1, d_inner),
        ssm_norm_g.reshape(1, d_inner),
        jnp.concatenate([e_mat, e_mat], axis=0),
        w_out.astype(BF16),
    ]
    row_map = lambda b, s: (b * ns + s, 0)
    in_specs = [pl.BlockSpec((ts, d), row_map),
                _resident((1, d)),
                _resident((d, n_pad)),
                _resident((SSM_CONV, conv_dim)),
                _resident((1, conv_dim)),
                _resident((1, LANES)),
                _resident((1, LANES)),
                _resident((1, d_inner)),
                _resident((1, d_inner)),
                _resident((2 * LANES, d_inner)),
                _resident((d_inner, d))]
    return pl.pallas_call(
        functools.partial(_mamba_body, ts=ts, d_inner=d_inner, n_groups=n_groups, n_state=n_state,
                          n_heads=heads),
        grid=(batch, ns),
        in_specs=in_specs,
        out_specs=pl.BlockSpec((ts, d), row_map),
        out_shape=jax.ShapeDtypeStruct((t, d), F32),
        scratch_shapes=[pltpu.VMEM((ts, d), BF16),
                        pltpu.VMEM((conv_dim // LANES, 2 * (ts + SUBLANES), LANES), F32),
                        pltpu.VMEM((ts, d_inner), F32),
                        pltpu.VMEM((ts, d_inner), BF16),
                        pltpu.VMEM((ts, 2 * n_groups * n_state), BF16),
                        pltpu.VMEM((ts, d_inner), F32),
                        pltpu.VMEM((ts, d_inner), BF16),
                        pltpu.VMEM((ts, d_inner), F32),
                        pltpu.VMEM((ts, d_inner), F32),
                        pltpu.VMEM((ts, LANES), F32),
                        pltpu.VMEM((LANES, ts), F32),
                        pltpu.VMEM((n_groups, n_state, d_inner // n_groups), F32)],
        compiler_params=pltpu.CompilerParams(
            dimension_semantics=("arbitrary", "arbitrary"), vmem_limit_bytes=VMEM_LIMIT_BYTES),
        name="mamba_mixer",
    )(*operands)


def _rope_rows(xb, cos, sin_signed):
    lane = lax.broadcasted_iota(jnp.int32, xb.shape, 1)
    first_half = (lane & (DIFF_HEAD_DIM - 1)) < (DIFF_HEAD_DIM // 2)
    rot = jnp.where(first_half,
                    pltpu.roll(xb, LANES - DIFF_HEAD_DIM // 2, 1),
                    pltpu.roll(xb, DIFF_HEAD_DIM // 2, 1))
    return xb * cos + rot * sin_signed


def _rope_cols(xt, cos_t, sin_signed_t):
    half = DIFF_HEAD_DIM // 2
    parts = []
    for u in range(xt.shape[0] // DIFF_HEAD_DIM):
        base = u * DIFF_HEAD_DIM
        parts += [xt[base + half:base + 2 * half], xt[base:base + half]]
    return xt * cos_t + jnp.concatenate(parts, axis=0) * sin_signed_t


def _kv_body(x_ref, g_ref, wk_ref, wvt_ref, cos_ref, sin_ref, k_ref, vt_ref, *, qk_dim):
    xn = _rms(x_ref[...], g_ref[...]).astype(BF16)
    cos = cos_ref[...]
    sin = sin_ref[...]
    k_all = _dot(xn, wk_ref[...])
    for j in range(qk_dim // LANES):
        k_ref[:, j * LANES:(j + 1) * LANES] = _rope_rows(k_all[:, j * LANES:(j + 1) * LANES], cos, sin).astype(BF16)
    hd = 2 * DIFF_HEAD_DIM
    ones = jnp.ones((V_ONES_ROWS, xn.shape[0]), BF16)
    vt = _dot_nt(wvt_ref[...], xn).astype(BF16)
    for h in range(wvt_ref.shape[0] // hd):
        base = h * (hd + V_ONES_ROWS)
        vt_ref[0, base:base + hd, :] = vt[h * hd:(h + 1) * hd]
        vt_ref[0, base + hd:base + hd + V_ONES_ROWS, :] = ones


def _shared_kv(x, norm_g, w_kv, cos_t, sin_t, *, qk_dim, seq, tk):
    t, d = x.shape
    v_dim = w_kv.shape[1] - qk_dim
    hd = 2 * DIFF_HEAD_DIM
    vt_rows = (v_dim // hd) * (hd + V_ONES_ROWS)
    n_seq_tiles = seq // tk
    return pl.pallas_call(
        functools.partial(_kv_body, qk_dim=qk_dim),
        grid=(t // tk,),
        in_specs=[pl.BlockSpec((tk, d), lambda i: (i, 0)),
                  _resident((1, d)),
                  _resident((d, qk_dim)),
                  _resident((v_dim, d)),
                  pl.BlockSpec((tk, LANES), lambda i: (i % n_seq_tiles, 0)),
                  pl.BlockSpec((tk, LANES), lambda i: (i % n_seq_tiles, 0))],
        out_specs=[pl.BlockSpec((tk, qk_dim), lambda i: (i, 0)),
                   pl.BlockSpec((1, vt_rows, tk), lambda i: (i, 0, 0))],
        out_shape=[jax.ShapeDtypeStruct((t, qk_dim), BF16),
                   jax.ShapeDtypeStruct((t // tk, vt_rows, tk), BF16)],
        compiler_params=pltpu.CompilerParams(
            dimension_semantics=("arbitrary",), vmem_limit_bytes=VMEM_LIMIT_BYTES),
        name="shared_kv",
    )(x, norm_g.reshape(1, d), w_kv[:, :qk_dim].astype(BF16), w_kv[:, qk_dim:].T.astype(BF16), cos_t, sin_t)


def _attn_body(x_ref, g_ref, wqt_ref, cos_ref, sin_ref, k_ref, vt_ref, lam_ref, sg_ref, wo_ref, o_ref,
               qt_ref, sc_ref, m_ref, acc_ref, os_ref, *, tq, n_heads, lambda_init):
    qi = pl.program_id(1)
    hd = 2 * DIFF_HEAD_DIM
    hv = hd + V_ONES_ROWS
    scale = DIFF_HEAD_DIM ** -0.5 * math.log2(math.e)

    xn = _rms(x_ref[...], g_ref[...]).astype(BF16)
    cos = cos_ref[...]
    sin = sin_ref[...]
    feat = lax.broadcasted_iota(jnp.int32, (hd, tq), 0)
    map0 = feat < DIFF_HEAD_DIM
    qt_all = _dot_nt(wqt_ref[...], xn)
    for h in range(n_heads):
        qt = _rope_cols(qt_all[h * hd:(h + 1) * hd], cos, sin) * scale
        qt_ref[h] = jnp.concatenate([jnp.where(map0, qt, 0.0), jnp.where(map0, 0.0, qt)], axis=1).astype(BF16)

    lam = lam_ref[...]
    t1 = jnp.sum(lam[0:1] * lam[1:2], axis=1, keepdims=True)
    t2 = jnp.sum(lam[2:3] * lam[3:4], axis=1, keepdims=True)
    lam_full = jnp.exp(t1) - jnp.exp(t2) + lambda_init

    k_chunk = lax.broadcasted_iota(jnp.int32, (tq, 2 * tq), 0) >> CHUNK_SHIFT
    q_chunk = (lax.broadcasted_iota(jnp.int32, (tq, 2 * tq), 1) & (tq - 1)) >> CHUNK_SHIFT
    diag_mask = k_chunk <= q_chunk

    def score(j, h):
        rows = pl.ds(pl.multiple_of(j * tq, tq), tq)
        return _dot(k_ref[rows, h * hd:(h + 1) * hd], qt_ref[h])

    m_ref[...] = jnp.full(m_ref.shape, -jnp.inf, F32)
    acc_ref[...] = jnp.zeros(acc_ref.shape, F32)

    for h in range(SCORE_LOOKAHEAD):
        sc_ref[h] = score(0, h)

    def kv_block(j, masked):
        scores = {}
        for h in range(n_heads):
            nxt = h + SCORE_LOOKAHEAD
            if nxt < n_heads:
                scores[nxt] = score(j, nxt)
            elif not masked:
                sc_ref[nxt - n_heads] = score(j + 1, nxt - n_heads)
            s = sc_ref[h] if h < SCORE_LOOKAHEAD else scores.pop(h)
            if masked:
                s = jnp.where(diag_mask, s, -jnp.inf)
            m_prev = m_ref[h]
            m_next = jnp.maximum(m_prev, jnp.max(s, axis=0, keepdims=True))
            p = jnp.exp2(s - m_next)
            alpha = jnp.exp2(m_prev - m_next)
            m_ref[h] = m_next
            acc_ref[h] = alpha * acc_ref[h] + _dot(vt_ref[j, h * hv:(h + 1) * hv, :], p.astype(BF16))

    def loop_body(j, carry):
        kv_block(j, False)
        return carry

    lax.fori_loop(0, qi, loop_body, 0)
    kv_block(qi, True)

    sg = jnp.concatenate([sg_ref[...]] * (tq // LANES), axis=1)
    for h in range(n_heads):
        o_all = acc_ref[h, 0:hd, :] * (1.0 / acc_ref[h, hd:hd + 1, :])
        o = o_all[:, 0:tq] - lam_full * o_all[:, tq:2 * tq]
        ms = jnp.mean(o * o, axis=0, keepdims=True)
        o = o * lax.rsqrt(ms + EPS) * sg * (1.0 - lambda_init)
        os_ref[:, h * hd:(h + 1) * hd] = o.T.astype(BF16)

    o_ref[...] = x_ref[...] + _dot(os_ref[...], wo_ref[...])


def _diff_attn(x, norm_g, w_q, cos_tt, sin_tt, k, vt, lam, subln_g, w_o, *, batch, lambda_init, tq):
    t, d = x.shape
    seq = t // batch
    assert seq % tq == 0 and tq % LANES == 0 and (tq & (tq - 1)) == 0 and tq % CHUNK == 0
    qk_dim = w_q.shape[1]
    v_dim = w_o.shape[0]
    hd = 2 * DIFF_HEAD_DIM
    hv = hd + V_ONES_ROWS
    n_heads = qk_dim // hd
    assert vt.shape[1] == n_heads * hv and v_dim == n_heads * hd
    nq = seq // tq
    row_map = lambda b, i: (b * nq + i, 0)
    sg_rows = jnp.broadcast_to(subln_g.reshape(hd, 1), (hd, LANES))
    return pl.pallas_call(
        functools.partial(_attn_body, tq=tq, n_heads=n_heads, lambda_init=lambda_init),
        grid=(batch, nq),
        in_specs=[pl.BlockSpec((tq, d), row_map),
                  _resident((1, d)),
                  _resident((qk_dim, d)),
                  pl.BlockSpec((hd, tq), lambda b, i: (0, i)),
                  pl.BlockSpec((hd, tq), lambda b, i: (0, i)),
                  pl.BlockSpec((seq, qk_dim), lambda b, i: (b, 0)),
                  pl.BlockSpec((nq, n_heads * hv, tq), lambda b, i: (b, 0, 0)),
                  _resident(lam.shape),
                  _resident((hd, LANES)),
                  _resident((v_dim, d))],
        out_specs=pl.BlockSpec((tq, d), row_map),
        out_shape=jax.ShapeDtypeStruct((t, d), F32),
        scratch_shapes=[pltpu.VMEM((n_heads, hd, 2 * tq), BF16),
                        pltpu.VMEM((SCORE_LOOKAHEAD, tq, 2 * tq), F32),
                        pltpu.VMEM((n_heads, 1, 2 * tq), F32),
                        pltpu.VMEM((n_heads, hv, 2 * tq), F32),
                        pltpu.VMEM((tq, n_heads * hd), BF16)],
        compiler_params=pltpu.CompilerParams(
            dimension_semantics=("arbitrary", "arbitrary"), vmem_limit_bytes=VMEM_LIMIT_BYTES),
        name="diff_attn",
    )(x, norm_g.reshape(1, d), w_q.T.astype(BF16), cos_tt, sin_tt, k, vt, lam, sg_rows, w_o.astype(BF16))


def _rope_tables(seq):
    dim = DIFF_HEAD_DIM
    inv = ROPE_THETA ** (-jnp.arange(0, dim, 2, dtype=F32) / dim)
    ang = jnp.arange(seq, dtype=F32)[:, None] * inv[None, :]
    cos, sin = jnp.cos(ang), jnp.sin(ang)
    cos_t = jnp.tile(cos, (1, 2 * LANES // dim))
    sin_t = jnp.tile(jnp.concatenate([-sin, sin], axis=1), (1, LANES // dim))
    return cos_t, sin_t


def kernel(x, ffn1_norm, ffn1_w_in, ffn1_w_out, mixer_norm, ffn2_norm, ffn2_w_in, ffn2_w_out, ssm_w_in, ssm_conv_w, ssm_conv_b, ssm_dt_bias, ssm_a_log, ssm_d, ssm_norm, ssm_w_out, kv_norm, w_kv, attn_w_q, attn_lambda, attn_subln, attn_w_o, final_norm):
    batch, seq, d = x.shape
    depth = ffn1_norm.shape[0]
    n_a = ssm_w_in.shape[0]
    qk_dim = attn_w_q.shape[2]
    tq = min(256, seq)
    cos_t, sin_t = _rope_tables(seq)
    cos_tt, sin_tt = cos_t.T, sin_t.T
    h = x.reshape(batch * seq, d)
    k = vt = None
    ffn1_in, ffn1_out = ffn1_w_in.astype(BF16), ffn1_w_out.astype(BF16)
    ffn2_in, ffn2_out = ffn2_w_in.astype(BF16), ffn2_w_out.astype(BF16)
    for layer in range(depth):
        h = _ffn(h, ffn1_norm[layer], ffn1_in, ffn1_out, layer)
        if layer < n_a:
            i = layer
            h = _mamba_mixer(h, mixer_norm[layer], ssm_w_in[i], ssm_conv_w[i], ssm_conv_b[i], ssm_dt_bias[i],
                             ssm_a_log[i], ssm_d[i], ssm_norm[i], ssm_w_out[i], batch=batch)
        else:
            j = layer - n_a
            lambda_init = 0.8 - 0.6 * math.exp(-0.3 * layer)
            h = _diff_attn(h, mixer_norm[layer], attn_w_q[j], cos_tt, sin_tt, k, vt, attn_lambda[j],
                           attn_subln[j], attn_w_o[j], batch=batch, lambda_init=lambda_init, tq=tq)
        last = layer == depth - 1
        h = _ffn(h, ffn2_norm[layer], ffn2_in, ffn2_out, layer, final_norm if last else None)
        if layer == n_a - 1:
            k, vt = _shared_kv(h, kv_norm, w_kv, cos_t, sin_t, qk_dim=qk_dim, seq=seq, tk=tq)
    return h.reshape(batch, seq, d)
```
